```python
import jax, jax.numpy as jnp
from jax import lax
import numpy as np

D_MODEL = 2048
BATCH = 1
SEQ = 16384
DEPTH = 2

GRID_W = 64
CTX_LEN = 256
EPS = 1e-6

D_MIX = D_MODEL
D_LRU = D_MIX // 2
LRU_HEADS = 16
LRU_HEAD_DIM = D_LRU // LRU_HEADS
LRU_C = 8.0
LRU_CONV = 4
D_CONV = D_MIX // 4
CONV_WIDTH = 31
D_SGU = D_MIX // 4
SGU_HEADS = 8
SGU_HEAD_DIM = D_SGU // SGU_HEADS
CHUNK = 128
ROWS_PER_CHUNK = CHUNK // GRID_W

O_LRU_G = D_LRU
O_CV = 2 * D_LRU
O_CV_G = O_CV + 2 * D_CONV
O_SGU = O_CV_G + D_CONV
O_SGU_G = O_SGU + 2 * D_SGU
D_IN = O_SGU_G + D_SGU

kernel_name = "hybrid_lru_conformer_sgu_prefix_dit"


def rms_norm(x, g):
    xf = x.astype(jnp.float32)
    y = xf * lax.rsqrt(jnp.mean(xf * xf, axis=-1, keepdims=True) + EPS)
    return (y * g.astype(jnp.float32)).astype(x.dtype)


def layer_norm(x, g, b):
    xf = x.astype(jnp.float32)
    mu = jnp.mean(xf, axis=-1, keepdims=True)
    var = jnp.mean(jnp.square(xf - mu), axis=-1, keepdims=True)
    y = (xf - mu) * lax.rsqrt(var + EPS) * g.astype(jnp.float32) + b.astype(jnp.float32)
    return y.astype(x.dtype)


def depthwise_conv(x, w, b, pad):
    y = lax.conv_general_dilated(
        x, w[:, None, :].astype(x.dtype), window_strides=(1,), padding=[pad],
        dimension_numbers=("NWC", "WIO", "NWC"), feature_group_count=x.shape[-1])
    return y + b.astype(x.dtype)


def rglru_coeffs(xc, lam, w_r, b_r, w_i, b_i):
    bsz, length, _ = xc.shape
    xh = xc.reshape(bsz, length, LRU_HEADS, LRU_HEAD_DIM).astype(jnp.float32)
    r = jax.nn.sigmoid(jnp.einsum("blhi,hij->blhj", xh, w_r.astype(jnp.float32)) + b_r.astype(jnp.float32))
    i = jax.nn.sigmoid(jnp.einsum("blhi,hij->blhj", xh, w_i.astype(jnp.float32)) + b_i.astype(jnp.float32))
    log_a = -LRU_C * jax.nn.softplus(-lam.astype(jnp.float32).reshape(LRU_HEADS, LRU_HEAD_DIM)) * r
    a = jnp.exp(log_a)
    mult = jnp.sqrt(-jnp.expm1(2.0 * log_a))
    bb = mult * (i * xh)
    return a.reshape(bsz, length, D_LRU), bb.reshape(bsz, length, D_LRU)


def linear_scan(a, b, h0, reverse):
    if h0 is not None:
        idx = -1 if reverse else 0
        b = b.at[:, idx].add(a[:, idx] * h0)

    def combine(lhs, rhs):
        return (lhs[0] * rhs[0], rhs[0] * lhs[1] + rhs[1])

    _, h = lax.associative_scan(combine, (a, b), axis=1, reverse=reverse)
    return h


def rglru_mixer(xa_lat, xa_ctx, conv_w, conv_b, lam, w_r, b_r, w_i, b_i, need_ctx):
    pad = (LRU_CONV // 2, LRU_CONV - 1 - LRU_CONV // 2)
    xl = depthwise_conv(xa_lat, conv_w, conv_b, pad)
    xc = depthwise_conv(xa_ctx, conv_w, conv_b, pad)
    y_lat = None
    y_ctx = None
    for d, rev in enumerate((False, True)):
        a_c, b_c = rglru_coeffs(xc, lam[d], w_r[d], b_r[d], w_i[d], b_i[d])
        h_c = linear_scan(a_c, b_c, None, rev)
        h_end = h_c[:, 0] if rev else h_c[:, -1]
        a_l, b_l = rglru_coeffs(xl, lam[d], w_r[d], b_r[d], w_i[d], b_i[d])
        h_l = linear_scan(a_l, b_l, h_end, rev)
        y_lat = h_l if y_lat is None else y_lat + h_l
        if need_ctx:
            y_ctx = h_c if y_ctx is None else y_ctx + h_c
    y_lat = y_lat.astype(xa_lat.dtype)
    if need_ctx:
        y_ctx = y_ctx.astype(xa_ctx.dtype)
    return y_lat, y_ctx


def conformer_conv(p, w, b, ln_g, ln_b):
    z = p[..., :D_CONV] * jax.nn.sigmoid(p[..., D_CONV:])
    z = depthwise_conv(z, w, b, (CONV_WIDTH // 2, CONV_WIDTH // 2))
    return jax.nn.silu(layer_norm(z, ln_g, ln_b))


def chunk_sgu(p, n_chunks, ln_g, ln_b, w_s, b_s):
    bsz, length, _ = p.shape
    z = jax.nn.gelu(p)
    u, v = z[..., :D_SGU], z[..., D_SGU:]
    v = layer_norm(v, ln_g, ln_b)
    vh = v.reshape(bsz, n_chunks, CHUNK, SGU_HEADS, SGU_HEAD_DIM)
    s = jnp.einsum("hqp,bnphd->bnqhd", w_s.astype(vh.dtype), vh)
    s = s + jnp.swapaxes(b_s, 0, 1).astype(vh.dtype)[:, :, None]
    return u * s.reshape(bsz, length, D_SGU)


def mix_out(p, y_lru, n_chunks, w_out, cv_w, cv_b, cv_ln_g, cv_ln_b, sgu_ln_g, sgu_ln_b, sgu_w, sgu_b):
    y_cv = conformer_conv(p[..., O_CV:O_CV_G], cv_w, cv_b, cv_ln_g, cv_ln_b)
    y_sg = chunk_sgu(p[..., O_SGU:O_SGU_G], n_chunks, sgu_ln_g, sgu_ln_b, sgu_w, sgu_b)
    y = jnp.concatenate([
        y_lru * jax.nn.silu(p[..., O_LRU_G:O_CV]),
        y_cv * jax.nn.silu(p[..., O_CV_G:O_SGU]),
        y_sg * jax.nn.silu(p[..., O_SGU_G:D_IN]),
    ], axis=-1)
    return y @ w_out


def setup_inputs(seed: int = 0) -> dict:
    key = jax.random.key(seed)
    ks = jax.random.split(key, 32)
    f32 = jnp.float32
    nrm = lambda k, shape, s: jax.random.normal(k, shape, f32) * s
    u = jax.random.uniform(ks[10], (DEPTH, 2, D_LRU), f32, minval=0.9, maxval=0.999)
    sg = u ** (1.0 / LRU_C)
    lam = jnp.log(sg) - jnp.log1p(-sg)
    return {
        "x": nrm(ks[0], (BATCH, SEQ, D_MODEL), 1.0),
        "c": nrm(ks[1], (BATCH, D_MODEL), 1.0),
        "ctx": nrm(ks[2], (BATCH, CTX_LEN, D_MODEL), 1.0),
        "c_ctx": nrm(ks[3], (D_MODEL,), 1.0),
        "w_mod": nrm(ks[4], (DEPTH, D_MODEL, 3 * D_MODEL), 0.5 * D_MODEL ** -0.5),
        "b_mod": nrm(ks[5], (DEPTH, 3 * D_MODEL), 0.01),
        "norm_g": 1.0 + nrm(ks[6], (DEPTH, D_MODEL), 0.01),
        "w_in": nrm(ks[7], (DEPTH, D_MODEL, D_IN), D_MODEL ** -0.5),
        "w_out": nrm(ks[8], (DEPTH, D_MIX, D_MODEL), D_MIX ** -0.5),
        "lru_conv_w": nrm(ks[9], (DEPTH, LRU_CONV, D_LRU), LRU_CONV ** -0.5),
        "lru_conv_b": nrm(ks[11], (DEPTH, D_LRU), 0.01),
        "lru_lam": lam,
        "lru_w_r": nrm(ks[12], (DEPTH, 2, LRU_HEADS, LRU_HEAD_DIM, LRU_HEAD_DIM), LRU_HEAD_DIM ** -0.5),
        "lru_b_r": nrm(ks[13], (DEPTH, 2, LRU_HEADS, LRU_HEAD_DIM), 0.01),
        "lru_w_i": nrm(ks[14], (DEPTH, 2, LRU_HEADS, LRU_HEAD_DIM, LRU_HEAD_DIM), LRU_HEAD_DIM ** -0.5),
        "lru_b_i": nrm(ks[15], (DEPTH, 2, LRU_HEADS, LRU_HEAD_DIM), 0.01),
        "cv_w": nrm(ks[16], (DEPTH, CONV_WIDTH, D_CONV), CONV_WIDTH ** -0.5),
        "cv_b": nrm(ks[17], (DEPTH, D_CONV), 0.01),
        "cv_ln_g": 1.0 + nrm(ks[18], (DEPTH, D_CONV), 0.01),
        "cv_ln_b": nrm(ks[19], (DEPTH, D_CONV), 0.01),
        "sgu_ln_g": 1.0 + nrm(ks[20], (DEPTH, D_SGU), 0.01),
        "sgu_ln_b": nrm(ks[21], (DEPTH, D_SGU), 0.01),
        "sgu_w": nrm(ks[22], (DEPTH, SGU_HEADS, CHUNK, CHUNK), CHUNK ** -0.5),
        "sgu_b": 1.0 + nrm(ks[23], (DEPTH, SGU_HEADS, CHUNK), 0.01),
        "final_g": 1.0 + nrm(ks[24], (D_MODEL,), 0.01),
    }


def reference(x, c, ctx, c_ctx, w_mod, b_mod, norm_g, w_in, w_out, lru_conv_w, lru_conv_b,
              lru_lam, lru_w_r, lru_b_r, lru_w_i, lru_b_i, cv_w, cv_b, cv_ln_g, cv_ln_b,
              sgu_ln_g, sgu_ln_b, sgu_w, sgu_b, final_g):
    rows = x.shape[1] // GRID_W
    lat_chunks = rows // ROWS_PER_CHUNK
    ctx_chunks = ctx.shape[1] // CHUNK
    c_act = jax.nn.silu(c)
    cc_act = jax.nn.silu(c_ctx)
    for l in range(DEPTH):
        last = l == DEPTH - 1
        mod_x = c_act @ w_mod[l] + b_mod[l]
        mod_c = cc_act @ w_mod[l] + b_mod[l]
        sh_x, sc_x, g_x = jnp.split(mod_x, 3, axis=-1)
        sh_c, sc_c, g_c = jnp.split(mod_c, 3, axis=-1)
        hx = rms_norm(x, norm_g[l]) * (1.0 + sc_x[:, None]) + sh_x[:, None]
        hc = rms_norm(ctx, norm_g[l]) * (1.0 + sc_c) + sh_c
        px = hx @ w_in[l]
        pc = hc @ (w_in[l][:, :D_LRU] if last else w_in[l])
        y_lru_x, y_lru_c = rglru_mixer(
            px[..., :D_LRU], pc[..., :D_LRU], lru_conv_w[l], lru_conv_b[l], lru_lam[l],
            lru_w_r[l], lru_b_r[l], lru_w_i[l], lru_b_i[l], need_ctx=not last)
        x = x + g_x[:, None] * mix_out(px, y_lru_x, lat_chunks, w_out[l], cv_w[l], cv_b[l],
                                       cv_ln_g[l], cv_ln_b[l], sgu_ln_g[l], sgu_ln_b[l],
                                       sgu_w[l], sgu_b[l])
        if not last:
            ctx = ctx + g_c * mix_out(pc, y_lru_c, ctx_chunks, w_out[l], cv_w[l], cv_b[l],
                                      cv_ln_g[l], cv_ln_b[l], sgu_ln_g[l], sgu_ln_b[l],
                                      sgu_w[l], sgu_b[l])
    return rms_norm(x, final_g)
```

```python
import functools

import jax
import jax.numpy as jnp
from jax import lax
from jax.experimental import pallas as pl
from jax.experimental.pallas import tpu as pltpu

F32 = jnp.float32
BF16 = jnp.bfloat16

D_MODEL = 2048
EPS = 1e-6
D_LRU = 1024
LRU_HEADS = 16
LRU_HEAD_DIM = D_LRU // LRU_HEADS
LRU_C = 8.0
LRU_CONV = 4
D_CONV = 512
CONV_WIDTH = 31
D_SGU = 512
SGU_HEADS = 8
CHUNK = 128
D_IN = 5120
D_EXT = D_LRU + 2 * D_CONV

LANES = 128
SUBLANES = 8
HALO = 16
TILE = 256
GATE_BLK = 256
VMEM_LIMIT = 56 * 1024 * 1024


def _sigmoid(v):
    return 1.0 / (1.0 + jnp.exp(-v))


def _silu(v):
    return v * _sigmoid(v)


def _scan_pitch(sub_len):
    p = sub_len if (sub_len // SUBLANES) % 2 == 1 else sub_len + SUBLANES
    return p


def _fill_scan(a_s, b_s, a_val, b_val, col0, sub_len, pitch):
    w = a_val.shape[1]
    for q in range(w // LANES):
        k = col0 // LANES + q
        for s in range(SUBLANES):
            rows = slice(s * sub_len, (s + 1) * sub_len)
            cols = slice(q * LANES, (q + 1) * LANES)
            a_s[k, s * pitch:s * pitch + sub_len, :] = a_val[rows, cols]
            b_s[k, s * pitch:s * pitch + sub_len, :] = b_val[rows, cols]


def _scan_tile(a_s, b_s, carry_ref, out_ref, sub_len, pitch, reverse):
    n_slab = D_LRU // LANES

    def step(jj, carry):
        hs, ps = carry
        j = (sub_len - 1 - jj) if reverse else jj
        new_h, new_p = [], []
        for k in range(n_slab):
            a = a_s[k, pl.ds(j, SUBLANES, stride=pitch), :]
            b = b_s[k, pl.ds(j, SUBLANES, stride=pitch), :]
            h = a * hs[k] + b
            p = a * ps[k]
            b_s[k, pl.ds(j, SUBLANES, stride=pitch), :] = h
            a_s[k, pl.ds(j, SUBLANES, stride=pitch), :] = p
            new_h.append(h)
            new_p.append(p)
        return tuple(new_h), tuple(new_p)

    zeros = tuple(jnp.zeros((SUBLANES, LANES), F32) for _ in range(n_slab))
    ones = tuple(jnp.ones((SUBLANES, LANES), F32) for _ in range(n_slab))
    h_end, p_end = lax.fori_loop(0, sub_len, step, (zeros, ones))

    order = range(SUBLANES - 1, -1, -1) if reverse else range(SUBLANES)
    for k in range(n_slab):
        cols = slice(k * LANES, (k + 1) * LANES)
        c = carry_ref[:, cols]
        for s in order:
            rows = slice(s * pitch, s * pitch + sub_len)
            out_ref[s * sub_len:(s + 1) * sub_len, cols] = b_s[k, rows, :] + a_s[k, rows, :] * c
            c = h_end[k][s:s + 1, :] + p_end[k][s:s + 1, :] * c
        carry_ref[:, cols] = c


def _mod_kernel(c_ref, w_ref, b_ref, o_ref):
    act = _silu(c_ref[...]).astype(BF16)
    o_ref[0] = jnp.dot(act, w_ref[0].astype(BF16), preferred_element_type=F32) + b_ref[0]


def _pass_a_kernel(x_ref, xp_ref, xn_ref, mod_ref, ng_ref, w_ref, lcw_ref, lcb_ref, lam_ref,
                   wg_ref, bg_ref, cvw_ref, cvb_ref, cvg_ref, cvbt_ref, sgg_ref, sgbt_ref,
                   sw_ref, sbias_ref, h0_ref,
                   hf_ref, ab_ref, bb_ref, gl_ref, yg_ref, hend_ref,
                   hn_s, pext_s, z_s, a_s, b_s, carry_s, *, tile, sub_len, pitch):
    i = pl.program_id(0)
    n = pl.num_programs(0)
    ext = tile + 2 * HALO

    shift = mod_ref[:, 0:D_MODEL]
    gain = ng_ref[...] * (1.0 + mod_ref[:, D_MODEL:2 * D_MODEL])

    def norm_mod(v):
        ms = jnp.mean(v * v, axis=-1, keepdims=True)
        return (v * lax.rsqrt(ms + EPS) * gain + shift).astype(BF16)

    hn_s[0:HALO, :] = norm_mod(xp_ref[...])
    hn_s[HALO:HALO + tile, :] = norm_mod(x_ref[...])
    hn_s[HALO + tile:ext, :] = norm_mod(xn_ref[...])

    pext_s[...] = jnp.dot(hn_s[...], w_ref[:, 0:D_EXT], preferred_element_type=F32)

    @pl.when(i == 0)
    def _():
        pext_s[0:HALO, :] = jnp.zeros((HALO, D_EXT), F32)

    @pl.when(i == n - 1)
    def _():
        pext_s[HALO + tile:ext, :] = jnp.zeros((HALO, D_EXT), F32)

    @pl.when(i == 0)
    def _():
        carry_s[...] = h0_ref[...]

    neg_lam = -lam_ref[...]
    softplus = jnp.maximum(neg_lam, 0.0) + jnp.log1p(jnp.exp(-jnp.abs(neg_lam)))
    c_lam = -LRU_C * softplus
    for j in range(D_LRU // GATE_BLK):
        cs = slice(GATE_BLK * j, GATE_BLK * (j + 1))
        xl = lcb_ref[:, cs]
        for k in range(LRU_CONV):
            r0 = HALO - LRU_CONV // 2 + k
            xl = xl + lcw_ref[k:k + 1, cs] * pext_s[r0:r0 + tile, cs]
        g = jnp.dot(xl.astype(BF16), wg_ref[j], preferred_element_type=F32)
        for d in range(2):
            g0 = 2 * GATE_BLK * d
            r = _sigmoid(g[:, g0:g0 + GATE_BLK] + bg_ref[2 * d:2 * d + 1, cs])
            gi = _sigmoid(g[:, g0 + GATE_BLK:g0 + 2 * GATE_BLK] + bg_ref[2 * d + 1:2 * d + 2, cs])
            log_a = c_lam[d:d + 1, cs] * r
            a = jnp.exp(log_a)
            bv = jnp.sqrt(-jnp.tanh(log_a) * (a * a + 1.0)) * (gi * xl)
            if d == 0:
                _fill_scan(a_s, b_s, a, bv, GATE_BLK * j, sub_len, pitch)
            else:
                ab_ref[:, cs] = a
                bb_ref[:, cs] = bv

    _scan_tile(a_s, b_s, carry_s, hf_ref, sub_len, pitch, reverse=False)
    hend_ref[...] = carry_s[...]

    hn_main = hn_s.at[HALO:HALO + tile, :]
    col = D_EXT
    gl_ref[...] = _silu(jnp.dot(hn_main[...], w_ref[:, col:col + D_LRU], preferred_element_type=F32))
    col += D_LRU

    z_s[...] = pext_s[:, D_LRU:D_LRU + D_CONV] * _sigmoid(pext_s[:, D_LRU + D_CONV:D_EXT])
    acc = cvb_ref[...]
    for k in range(CONV_WIDTH):
        r0 = HALO - CONV_WIDTH // 2 + k
        acc = acc + cvw_ref[k:k + 1, :] * z_s[r0:r0 + tile, :]
    mu = jnp.mean(acc, axis=-1, keepdims=True)
    cen = acc - mu
    var = jnp.mean(cen * cen, axis=-1, keepdims=True)
    y_cv = _silu(cen * lax.rsqrt(var + EPS) * cvg_ref[...] + cvbt_ref[...])
    g_cv = jnp.dot(hn_main[...], w_ref[:, col:col + D_CONV], preferred_element_type=F32)
    col += D_CONV
    yg_ref[:, 0:D_CONV] = (y_cv * _silu(g_cv)).astype(BF16)

    p_sg = jnp.dot(hn_main[...], w_ref[:, col:col + 3 * D_SGU], preferred_element_type=F32)
    zz = jax.nn.gelu(p_sg[:, 0:2 * D_SGU])
    u = zz[:, 0:D_SGU]
    v = zz[:, D_SGU:2 * D_SGU]
    mu = jnp.mean(v, axis=-1, keepdims=True)
    cen = v - mu
    var = jnp.mean(cen * cen, axis=-1, keepdims=True)
    vn = (cen * lax.rsqrt(var + EPS) * sgg_ref[...] + sgbt_ref[...]).astype(BF16)
    n_ch = tile // CHUNK
    lane = lax.broadcasted_iota(jnp.int32, (CHUNK, n_ch * LANES), 1) % LANES
    even_head = lane < (D_SGU // SGU_HEADS)
    g_sg = _silu(p_sg[:, 2 * D_SGU:3 * D_SGU])
    for k in range(D_SGU // LANES):
        cols = slice(LANES * k, LANES * (k + 1))
        rhs = jnp.concatenate([vn[c * CHUNK:(c + 1) * CHUNK, cols] for c in range(n_ch)], axis=1)
        s_even = jnp.dot(sw_ref[2 * k], rhs, preferred_element_type=F32)
        s_odd = jnp.dot(sw_ref[2 * k + 1], rhs, preferred_element_type=F32)
        sel = jnp.where(even_head, s_even, s_odd)
        for c in range(n_ch):
            rows = slice(c * CHUNK, (c + 1) * CHUNK)
            s_c = sel[:, c * LANES:(c + 1) * LANES] + sbias_ref[:, cols]
            yg_ref[rows, D_CONV + LANES * k:D_CONV + LANES * (k + 1)] = (
                u[rows, cols] * s_c * g_sg[rows, cols]).astype(BF16)


def _pass_b_kernel(hf_ref, ab_ref, bb_ref, gl_ref, yg_ref, x_ref, wo_ref, gate_ref, h0_ref, fg_ref,
                   xo_ref, hend_ref, a_s, b_s, hb_s, carry_s, *, sub_len, pitch, final_norm):
    i = pl.program_id(0)

    @pl.when(i == 0)
    def _():
        carry_s[...] = h0_ref[...]

    _fill_scan(a_s, b_s, ab_ref[...], bb_ref[...], 0, sub_len, pitch)
    _scan_tile(a_s, b_s, carry_s, hb_s, sub_len, pitch, reverse=True)
    hend_ref[...] = carry_s[...]

    y_lru = ((hf_ref[...] + hb_s[...]) * gl_ref[...]).astype(BF16)
    o = jnp.dot(y_lru, wo_ref[0:D_LRU, :], preferred_element_type=F32)
    o = o + jnp.dot(yg_ref[...], wo_ref[D_LRU:D_MODEL, :], preferred_element_type=F32)
    xn = x_ref[...] + gate_ref[...] * o
    if final_norm:
        ms = jnp.mean(xn * xn, axis=-1, keepdims=True)
        xn = xn * lax.rsqrt(ms + EPS) * fg_ref[...]
    xo_ref[...] = xn


def _const_spec(shape):
    nd = len(shape)
    return pl.BlockSpec(shape, lambda i: (0,) * nd, pipeline_mode=pl.Buffered(1))


def _mod_call(cvec, w_mod, b_mod):
    depth = w_mod.shape[0]
    bn = 768
    return pl.pallas_call(
        _mod_kernel,
        grid=(depth, 3 * D_MODEL // bn),
        in_specs=[
            pl.BlockSpec((SUBLANES, D_MODEL), lambda l, j: (0, 0)),
            pl.BlockSpec((1, D_MODEL, bn), lambda l, j: (l, 0, j)),
            pl.BlockSpec((1, 1, bn), lambda l, j: (l, 0, j)),
        ],
        out_specs=pl.BlockSpec((1, SUBLANES, bn), lambda l, j: (l, 0, j)),
        out_shape=jax.ShapeDtypeStruct((depth, SUBLANES, 3 * D_MODEL), F32),
        compiler_params=pltpu.CompilerParams(dimension_semantics=("arbitrary", "arbitrary")),
        name="mod_proj",
    )(cvec, w_mod, b_mod.reshape(depth, 1, 3 * D_MODEL))


def _pass_a_call(x, mod_row, h0, p):
    rows = x.shape[0]
    tile = min(TILE, rows)
    n = rows // tile
    sub_len = tile // SUBLANES
    pitch = _scan_pitch(sub_len)
    ext = tile + 2 * HALO
    hb = tile // HALO
    n_hb = rows // HALO
    row_spec = lambda w: pl.BlockSpec((tile, w), lambda i: (i, 0))
    consts = [mod_row, p["norm_g"], p["w_in"], p["lru_conv_w"], p["lru_conv_b"], p["lru_lam"],
              p["w_gate"], p["b_gate"], p["cv_w"], p["cv_b"], p["cv_ln_g"], p["cv_ln_b"],
              p["sgu_ln_g"], p["sgu_ln_b"], p["sgu_w"], p["sgu_bias"], h0]
    kern = functools.partial(_pass_a_kernel, tile=tile, sub_len=sub_len, pitch=pitch)
    slab = (D_LRU // LANES, SUBLANES * pitch, LANES)
    return pl.pallas_call(
        kern,
        grid=(n,),
        in_specs=[
            row_spec(D_MODEL),
            pl.BlockSpec((HALO, D_MODEL), lambda i: (jnp.maximum(i * hb - 1, 0), 0)),
            pl.BlockSpec((HALO, D_MODEL), lambda i: (jnp.minimum((i + 1) * hb, n_hb - 1), 0)),
        ] + [_const_spec(c.shape) for c in consts],
        out_specs=[row_spec(D_LRU), row_spec(D_LRU), row_spec(D_LRU), row_spec(D_LRU),
                   row_spec(D_LRU), pl.BlockSpec((1, D_LRU), lambda i: (0, 0))],
        out_shape=[
            jax.ShapeDtypeStruct((rows, D_LRU), F32),
            jax.ShapeDtypeStruct((rows, D_LRU), F32),
            jax.ShapeDtypeStruct((rows, D_LRU), F32),
            jax.ShapeDtypeStruct((rows, D_LRU), F32),
            jax.ShapeDtypeStruct((rows, D_LRU), BF16),
            jax.ShapeDtypeStruct((1, D_LRU), F32),
        ],
        scratch_shapes=[
            pltpu.VMEM((ext, D_MODEL), BF16),
            pltpu.VMEM((ext, D_EXT), F32),
            pltpu.VMEM((ext, D_CONV), F32),
            pltpu.VMEM(slab, F32),
            pltpu.VMEM(slab, F32),
            pltpu.VMEM((1, D_LRU), F32),
        ],
        compiler_params=pltpu.CompilerParams(
            dimension_semantics=("arbitrary",), vmem_limit_bytes=VMEM_LIMIT),
        name="pass_a",
    )(x, x, x, *consts)


def _pass_b_call(a_out, x, gate_row, h0, w_out, final_g, final_norm):
    hf, ab, bb, gl, yg = a_out
    rows = x.shape[0]
    tile = min(TILE, rows)
    n = rows // tile
    sub_len = tile // SUBLANES
    pitch = _scan_pitch(sub_len)
    row_spec = lambda w: pl.BlockSpec((tile, w), lambda i: (n - 1 - i, 0))
    consts = [w_out, gate_row, h0, final_g]
    kern = functools.partial(_pass_b_kernel, sub_len=sub_len, pitch=pitch, final_norm=final_norm)
    slab = (D_LRU // LANES, SUBLANES * pitch, LANES)
    return pl.pallas_call(
        kern,
        grid=(n,),
        in_specs=[row_spec(D_LRU)] * 5 + [row_spec(D_MODEL)] + [_const_spec(c.shape) for c in consts],
        out_specs=[row_spec(D_MODEL), pl.BlockSpec((1, D_LRU), lambda i: (0, 0))],
        out_shape=[jax.ShapeDtypeStruct((rows, D_MODEL), F32),
                   jax.ShapeDtypeStruct((1, D_LRU), F32)],
        scratch_shapes=[
            pltpu.VMEM(slab, F32),
            pltpu.VMEM(slab, F32),
            pltpu.VMEM((tile, D_LRU), F32),
            pltpu.VMEM((1, D_LRU), F32),
        ],
        compiler_params=pltpu.CompilerParams(
            dimension_semantics=("arbitrary",), vmem_limit_bytes=VMEM_LIMIT),
        name="pass_b",
    )(hf, ab, bb, gl, yg, x, *consts)


def _layer_params(l, w_in, w_out, norm_g, lru_conv_w, lru_conv_b, lru_lam, lru_w_r, lru_b_r,
                  lru_w_i, lru_b_i, cv_w, cv_b, cv_ln_g, cv_ln_b, sgu_ln_g, sgu_ln_b, sgu_w, sgu_b):
    w = w_in[l]
    o_cv, o_cvg = 2 * D_LRU, 2 * D_LRU + 2 * D_CONV
    w_perm = jnp.concatenate([w[:, 0:D_LRU], w[:, o_cv:o_cvg], w[:, D_LRU:o_cv], w[:, o_cvg:]], axis=1)
    gates = jnp.stack([lru_w_r[l, 0], lru_w_i[l, 0], lru_w_r[l, 1], lru_w_i[l, 1]])
    hpb = GATE_BLK // LRU_HEAD_DIM
    nb = D_LRU // GATE_BLK
    gates = gates.reshape(4, nb, hpb, LRU_HEAD_DIM, LRU_HEAD_DIM)
    bd = jnp.einsum("tjqio,qr->tjqiro", gates, jnp.eye(hpb, dtype=gates.dtype))
    bd = bd.reshape(4, nb, GATE_BLK, GATE_BLK).transpose(1, 2, 0, 3).reshape(nb, GATE_BLK, 4 * GATE_BLK)
    b_gate = jnp.stack([lru_b_r[l, 0], lru_b_i[l, 0], lru_b_r[l, 1], lru_b_i[l, 1]]).reshape(4, D_LRU)
    return dict(
        norm_g=norm_g[l].reshape(1, D_MODEL),
        w_in=w_perm.astype(BF16),
        w_out=w_out[l].astype(BF16),
        lru_conv_w=lru_conv_w[l],
        lru_conv_b=lru_conv_b[l].reshape(1, D_LRU),
        lru_lam=lru_lam[l],
        w_gate=bd.astype(BF16),
        b_gate=b_gate,
        cv_w=cv_w[l],
        cv_b=cv_b[l].reshape(1, D_CONV),
        cv_ln_g=cv_ln_g[l].reshape(1, D_CONV),
        cv_ln_b=cv_ln_b[l].reshape(1, D_CONV),
        sgu_ln_g=sgu_ln_g[l].reshape(1, D_SGU),
        sgu_ln_b=sgu_ln_b[l].reshape(1, D_SGU),
        sgu_w=sgu_w[l].astype(BF16),
        sgu_bias=jnp.repeat(sgu_b[l].T, D_SGU // SGU_HEADS, axis=1),
    )


def kernel(x, c, ctx, c_ctx, w_mod, b_mod, norm_g, w_in, w_out, lru_conv_w, lru_conv_b, lru_lam,
           lru_w_r, lru_b_r, lru_w_i, lru_b_i, cv_w, cv_b, cv_ln_g, cv_ln_b, sgu_ln_g, sgu_ln_b,
           sgu_w, sgu_b, final_g):
    assert x.shape[0] == 1 and c.shape[0] == 1 and ctx.shape[0] == 1
    depth = w_mod.shape[0]
    xs = x[0]
    cs = ctx[0]
    cvec = jnp.concatenate(
        [c, c_ctx.reshape(1, D_MODEL), jnp.zeros((SUBLANES - 2, D_MODEL), F32)], axis=0)
    mod = _mod_call(cvec, w_mod, b_mod)
    fg = final_g.reshape(1, D_MODEL)
    zero_state = jnp.zeros((1, D_LRU), F32)
    for l in range(depth):
        last = l == depth - 1
        p = _layer_params(l, w_in, w_out, norm_g, lru_conv_w, lru_conv_b, lru_lam, lru_w_r, lru_b_r,
                          lru_w_i, lru_b_i, cv_w, cv_b, cv_ln_g, cv_ln_b, sgu_ln_g, sgu_ln_b,
                          sgu_w, sgu_b)
        mod_x = mod[l, 0:1]
        mod_c = mod[l, 1:2]
        *a_ctx, hf_end = _pass_a_call(cs, mod_c[:, 0:2 * D_MODEL], zero_state, p)
        cs_new, hb_end = _pass_b_call(a_ctx, cs, mod_c[:, 2 * D_MODEL:], zero_state, p["w_out"], fg, False)
        *a_lat, _ = _pass_a_call(xs, mod_x[:, 0:2 * D_MODEL], hf_end, p)
        xs, _ = _pass_b_call(a_lat, xs, mod_x[:, 2 * D_MODEL:], hb_end, p["w_out"], fg, last)
        cs = cs_new
    return xs[None]
```

```python
import functools

import jax
import jax.numpy as jnp
from jax import lax
from jax.experimental import pallas as pl
from jax.experimental.pallas import tpu as pltpu

F32 = jnp.float32
BF16 = jnp.bfloat16

D_MODEL = 2048
EPS = 1e-6
D_LRU = 1024
LRU_HEADS = 16
LRU_HEAD_DIM = D_LRU // LRU_HEADS
LRU_C = 8.0
LRU_CONV = 4
D_CONV = 512
CONV_WIDTH = 31
D_SGU = 512
SGU_HEADS = 8
CHUNK = 128
D_IN = 5120
D_EXT = D_LRU + 2 * D_CONV
O_LRU_G = D_LRU
O_CV = 2 * D_LRU
O_CV_G = O_CV + 2 * D_CONV
O_SGU = O_CV_G + D_CONV
O_SGU_G = O_SGU + 2 * D_SGU
LOG2_E = 1.4426950408889634

LANES = 128
SUBLANES = 8
HALO = 16
LRU_HALO = 8
TILE = 256
GATE_BLK = 256
VMEM_LIMIT = 56 * 1024 * 1024


def _sigmoid(v):
    return 1.0 / (1.0 + jnp.exp2(v * (-LOG2_E)))


def _silu(v):
    hv = 0.5 * v
    return hv + hv * jnp.tanh(hv)


def _odd_pitch(rows):
    return rows if (rows // SUBLANES) % 2 == 1 else rows + SUBLANES


def _replicate_subchunks(src_ref, col0, n_slab, dst_s, row0, rows, sub_len, pitch):
    for q in range(n_slab):
        cols = slice(col0 + q * LANES, col0 + (q + 1) * LANES)
        for s in range(SUBLANES):
            r = row0 + s * sub_len
            dst_s[q, s * pitch:s * pitch + rows, :] = src_ref[r:r + rows, cols]


def _interleaved_conv(src_s, n_slab, pitch, first, taps, weight, bias, out_s, sub_len):
    for q in range(n_slab):
        cols = slice(q * LANES, (q + 1) * LANES)
        acc = [bias(q)] * sub_len
        for m in range(first, first + sub_len + taps - 1):
            v = src_s[q, pl.ds(m, SUBLANES, stride=pitch), :]
            for j in range(sub_len):
                k = m - first - j
                if 0 <= k < taps:
                    acc[j] = acc[j] + weight(k, q) * v
        for j in range(sub_len):
            out_s[SUBLANES * j:SUBLANES * (j + 1), cols] = acc[j]


def _to_natural(val_fn, n_slab, u_s, sub_len, pitch, emit):
    for q in range(n_slab):
        for j in range(sub_len):
            u_s[q, pl.ds(j, SUBLANES, stride=pitch), :] = val_fn(j, q)
        for s in range(SUBLANES):
            emit(s, q, u_s[q, s * pitch:s * pitch + sub_len, :])


def _scan_loop(a_load, b_load, h_store, p_store, sub_len, reverse):
    n_slab = D_LRU // LANES

    hs = [jnp.zeros((SUBLANES, LANES), F32)] * n_slab
    ps = [jnp.ones((SUBLANES, LANES), F32)] * n_slab
    for jj in range(sub_len):
        r = SUBLANES * ((sub_len - 1 - jj) if reverse else jj)
        for q in range(n_slab):
            cols = slice(q * LANES, (q + 1) * LANES)
            a = a_load(r, cols)
            hs[q] = a * hs[q] + b_load(r, cols)
            ps[q] = a * ps[q]
            h_store(r, cols, hs[q])
            p_store(r, cols, ps[q])
    return jnp.concatenate(hs, axis=1), jnp.concatenate(ps, axis=1)


def _chain_states(h_end, p_end, carry_ref, reverse):
    c = carry_ref[...]
    rows = [None] * SUBLANES
    order = range(SUBLANES - 1, -1, -1) if reverse else range(SUBLANES)
    for s in order:
        rows[s] = c
        c = h_end[s:s + 1, :] + p_end[s:s + 1, :] * c
    carry_ref[...] = c
    return jnp.concatenate(rows, axis=0)


def _mod_kernel(c_ref, w_ref, b_ref, o_ref):
    act = _silu(c_ref[...]).astype(BF16)
    o_ref[0] = jnp.dot(act, w_ref[0].astype(BF16), preferred_element_type=F32) + b_ref[0]


def _pass_a_kernel(x_ref, xp_ref, xn_ref, mod_ref, ng_ref, w_ref, lcw_ref, lcb_ref, lam_ref,
                   wg_ref, bg_ref, cvw_ref, cvb_ref, cvg_ref, cvbt_ref, sgg_ref, sgbt_ref,
                   sw_ref, sbias_ref, h0_ref,
                   hf_ref, ab_ref, bb_ref, gl_ref, yg_ref, hend_ref,
                   hn_s, pext_s, cl_s, xl_s, a_s, b_s, cz_s, w8_s, yc_s, u_s, carry_s,
                   *, tile, sub_len, pitch_l, pitch_z, pitch_u):
    i = pl.program_id(0)
    n = pl.num_programs(0)
    ext = tile + 2 * HALO

    @pl.when(i == 0)
    def _():
        carry_s[...] = h0_ref[...]
        for k in range(CONV_WIDTH):
            w8_s[k] = jnp.broadcast_to(cvw_ref[k:k + 1, :], (SUBLANES, D_CONV))

    shift = mod_ref[:, 0:D_MODEL]
    gain = ng_ref[...] * (1.0 + mod_ref[:, D_MODEL:2 * D_MODEL])

    def norm_mod(v, keep=None):
        ms = jnp.mean(v * v, axis=-1, keepdims=True)
        h = v * lax.rsqrt(ms + EPS) * gain + shift
        if keep is not None:
            h = h * keep
        return h.astype(BF16)

    keep_prev = jnp.where(i == 0, 0.0, 1.0).astype(F32)
    keep_next = jnp.where(i == n - 1, 0.0, 1.0).astype(F32)
    hn_s[0:HALO, :] = norm_mod(xp_ref[...], keep_prev)
    hn_s[HALO:HALO + tile, :] = norm_mod(x_ref[...])
    hn_s[HALO + tile:ext, :] = norm_mod(xn_ref[...], keep_next)

    pext_s[:, 0:D_LRU] = jnp.dot(hn_s[...], w_ref[:, 0:D_LRU], preferred_element_type=F32)
    pext_s[:, D_LRU:D_EXT] = jnp.dot(hn_s[...], w_ref[:, O_CV:O_CV_G], preferred_element_type=F32)

    n_slab = D_LRU // LANES
    _replicate_subchunks(pext_s, 0, n_slab, cl_s, HALO - LRU_HALO, sub_len + 2 * LRU_HALO,
                         sub_len, pitch_l)

    def lru_w(q):
        cols = slice(q * LANES, (q + 1) * LANES)
        return [jnp.broadcast_to(lcw_ref[k:k + 1, cols], (SUBLANES, LANES)) for k in range(LRU_CONV)]

    lru_w_cache = {}

    def lru_weight(k, q):
        if q not in lru_w_cache:
            lru_w_cache.clear()
            lru_w_cache[q] = lru_w(q)
        return lru_w_cache[q][k]

    _interleaved_conv(
        cl_s, n_slab, pitch_l, LRU_HALO - LRU_CONV // 2, LRU_CONV, lru_weight,
        lambda q: jnp.broadcast_to(lcb_ref[:, q * LANES:(q + 1) * LANES], (SUBLANES, LANES)),
        xl_s, sub_len)

    neg_lam = -lam_ref[...]
    softplus = jnp.maximum(neg_lam, 0.0) + jnp.log1p(jnp.exp(-jnp.abs(neg_lam)))
    c_lam = -LRU_C * softplus
    for j in range(D_LRU // GATE_BLK):
        cs = slice(GATE_BLK * j, GATE_BLK * (j + 1))
        xl = xl_s[:, cs]
        g = jnp.dot(xl.astype(BF16), wg_ref[j], preferred_element_type=F32)
        for d in range(2):
            g0 = 2 * GATE_BLK * d
            r = _sigmoid(g[:, g0:g0 + GATE_BLK] + bg_ref[2 * d:2 * d + 1, cs])
            gi = _sigmoid(g[:, g0 + GATE_BLK:g0 + 2 * GATE_BLK] + bg_ref[2 * d + 1:2 * d + 2, cs])
            log_a = c_lam[d:d + 1, cs] * r
            a = jnp.exp(log_a)
            bv = jnp.sqrt(-jnp.tanh(log_a) * (a * a + 1.0)) * (gi * xl)
            if d == 0:
                a_s[:, cs] = a
                b_s[:, cs] = bv
            else:
                ab_ref[:, cs] = a
                bb_ref[:, cs] = bv

    def st(ref):
        def store(r, cols, v):
            ref[pl.ds(r, SUBLANES), cols] = v
        return store

    ld = lambda ref: (lambda r, cols: ref[pl.ds(r, SUBLANES), cols])
    h_end, p_end = _scan_loop(ld(a_s), ld(b_s), st(b_s), st(a_s), sub_len, reverse=False)
    c_in = _chain_states(h_end, p_end, carry_s, reverse=False)
    hend_ref[...] = carry_s[...]
    for j in range(sub_len):
        rows = slice(SUBLANES * j, SUBLANES * (j + 1))
        hf_ref[rows, :] = b_s[rows, :] + a_s[rows, :] * c_in

    hn_main = hn_s.at[HALO:HALO + tile, :]
    gl_ref[...] = _silu(jnp.dot(hn_main[...], w_ref[:, O_LRU_G:O_CV], preferred_element_type=F32))

    n_slab = D_CONV // LANES
    for q in range(n_slab):
        ca = slice(D_LRU + q * LANES, D_LRU + (q + 1) * LANES)
        cg = slice(D_LRU + D_CONV + q * LANES, D_LRU + D_CONV + (q + 1) * LANES)
        for s in range(SUBLANES):
            r = s * sub_len
            rows = sub_len + 2 * HALO
            cz_s[q, s * pitch_z:s * pitch_z + rows, :] = (
                pext_s[r:r + rows, ca] * _sigmoid(pext_s[r:r + rows, cg]))
    _interleaved_conv(
        cz_s, n_slab, pitch_z, HALO - CONV_WIDTH // 2, CONV_WIDTH,
        lambda k, q: w8_s[k, :, q * LANES:(q + 1) * LANES],
        lambda q: jnp.broadcast_to(cvb_ref[:, q * LANES:(q + 1) * LANES], (SUBLANES, LANES)),
        yc_s, sub_len)
    acc = yc_s[...]
    mu = jnp.mean(acc, axis=-1, keepdims=True)
    cen = acc - mu
    var = jnp.mean(cen * cen, axis=-1, keepdims=True)
    yc_s[...] = _silu(cen * lax.rsqrt(var + EPS) * cvg_ref[...] + cvbt_ref[...])
    g_cv = _silu(jnp.dot(hn_main[...], w_ref[:, O_CV_G:O_SGU], preferred_element_type=F32))

    def emit_cv(s, q, block):
        rows = slice(s * sub_len, (s + 1) * sub_len)
        cols = slice(q * LANES, (q + 1) * LANES)
        yg_ref[rows, cols] = (block * g_cv[rows, cols]).astype(BF16)

    _to_natural(lambda j, q: yc_s[SUBLANES * j:SUBLANES * (j + 1), q * LANES:(q + 1) * LANES],
                n_slab, u_s, sub_len, pitch_u, emit_cv)

    p_sg = jnp.dot(hn_main[...], w_ref[:, O_SGU:D_IN], preferred_element_type=F32)
    zz = jax.nn.gelu(p_sg[:, 0:2 * D_SGU])
    u = zz[:, 0:D_SGU]
    v = zz[:, D_SGU:2 * D_SGU]
    mu = jnp.mean(v, axis=-1, keepdims=True)
    cen = v - mu
    var = jnp.mean(cen * cen, axis=-1, keepdims=True)
    vn = (cen * lax.rsqrt(var + EPS) * sgg_ref[...] + sgbt_ref[...]).astype(BF16)
    n_ch = tile // CHUNK
    lane = lax.broadcasted_iota(jnp.int32, (CHUNK, n_ch * LANES), 1) % LANES
    even_head = lane < (D_SGU // SGU_HEADS)
    g_sg = _silu(p_sg[:, 2 * D_SGU:3 * D_SGU])
    for k in range(D_SGU // LANES):
        cols = slice(LANES * k, LANES * (k + 1))
        rhs = jnp.concatenate([vn[c * CHUNK:(c + 1) * CHUNK, cols] for c in range(n_ch)], axis=1)
        s_even = jnp.dot(sw_ref[2 * k], rhs, preferred_element_type=F32)
        s_odd = jnp.dot(sw_ref[2 * k + 1], rhs, preferred_element_type=F32)
        sel = jnp.where(even_head, s_even, s_odd)
        for c in range(n_ch):
            rows = slice(c * CHUNK, (c + 1) * CHUNK)
            s_c = sel[:, c * LANES:(c + 1) * LANES] + sbias_ref[:, cols]
            yg_ref[rows, D_CONV + LANES * k:D_CONV + LANES * (k + 1)] = (
                u[rows, cols] * s_c * g_sg[rows, cols]).astype(BF16)


def _pass_b_kernel(hf_ref, ab_ref, bb_ref, gl_ref, yg_ref, x_ref, wo_ref, gate_ref, h0_ref, fg_ref,
                   xo_ref, hend_ref, hb_s, p_s, u_s, yl_s, carry_s,
                   *, tile, sub_len, pitch_u, final_norm):
    i = pl.program_id(0)

    @pl.when(i == 0)
    def _():
        carry_s[...] = h0_ref[...]

    def st(ref):
        def store(r, cols, v):
            ref[pl.ds(r, SUBLANES), cols] = v
        return store

    ld = lambda ref: (lambda r, cols: ref[pl.ds(r, SUBLANES), cols])
    h_end, p_end = _scan_loop(ld(ab_ref), ld(bb_ref), st(hb_s), st(p_s), sub_len, reverse=True)
    c_in = _chain_states(h_end, p_end, carry_s, reverse=True)
    hend_ref[...] = carry_s[...]

    def y_step(j, q):
        rows = slice(SUBLANES * j, SUBLANES * (j + 1))
        cols = slice(q * LANES, (q + 1) * LANES)
        return hf_ref[rows, cols] + hb_s[rows, cols] + p_s[rows, cols] * c_in[:, cols]

    def emit(s, q, block):
        rows = slice(s * sub_len, (s + 1) * sub_len)
        cols = slice(q * LANES, (q + 1) * LANES)
        yl_s[rows, cols] = (block * gl_ref[rows, cols]).astype(BF16)

    _to_natural(y_step, D_LRU // LANES, u_s, sub_len, pitch_u, emit)

    o = jnp.dot(yl_s[...], wo_ref[0:D_LRU, :], preferred_element_type=F32)
    o = o + jnp.dot(yg_ref[...], wo_ref[D_LRU:D_MODEL, :], preferred_element_type=F32)
    xn = x_ref[...] + gate_ref[...] * o
    if final_norm:
        ms = jnp.mean(xn * xn, axis=-1, keepdims=True)
        xn = xn * lax.rsqrt(ms + EPS) * fg_ref[...]
    xo_ref[...] = xn


def _const_spec(shape):
    nd = len(shape)
    return pl.BlockSpec(shape, lambda i: (0,) * nd, pipeline_mode=pl.Buffered(1))


def _mod_call(cvec, w_mod, b_mod):
    depth = w_mod.shape[0]
    bn = 768
    return pl.pallas_call(
        _mod_kernel,
        grid=(depth, 3 * D_MODEL // bn),
        in_specs=[
            pl.BlockSpec((SUBLANES, D_MODEL), lambda l, j: (0, 0)),
            pl.BlockSpec((1, D_MODEL, bn), lambda l, j: (l, 0, j)),
            pl.BlockSpec((1, 1, bn), lambda l, j: (l, 0, j)),
        ],
        out_specs=pl.BlockSpec((1, SUBLANES, bn), lambda l, j: (l, 0, j)),
        out_shape=jax.ShapeDtypeStruct((depth, SUBLANES, 3 * D_MODEL), F32),
        compiler_params=pltpu.CompilerParams(dimension_semantics=("arbitrary", "arbitrary")),
        name="mod_proj",
    )(cvec, w_mod, b_mod.reshape(depth, 1, 3 * D_MODEL))


def _tiling(rows):
    tile = min(TILE, rows)
    assert rows % tile == 0 and tile % (SUBLANES * SUBLANES) == 0 and tile % CHUNK == 0
    sub_len = tile // SUBLANES
    return tile, rows // tile, sub_len


def _pass_a_call(x, mod_row, h0, p):
    rows = x.shape[0]
    tile, n, sub_len = _tiling(rows)
    pitch_l = _odd_pitch(sub_len + 2 * LRU_HALO)
    pitch_z = _odd_pitch(sub_len + 2 * HALO)
    pitch_u = _odd_pitch(sub_len)
    ext = tile + 2 * HALO
    hb = tile // HALO
    n_hb = rows // HALO
    row_spec = lambda w: pl.BlockSpec((tile, w), lambda i: (i, 0))
    consts = [mod_row, p["norm_g"], p["w_in"], p["lru_conv_w"], p["lru_conv_b"], p["lru_lam"],
              p["w_gate"], p["b_gate"], p["cv_w"], p["cv_b"], p["cv_ln_g"], p["cv_ln_b"],
              p["sgu_ln_g"], p["sgu_ln_b"], p["sgu_w"], p["sgu_bias"], h0]
    kern = functools.partial(_pass_a_kernel, tile=tile, sub_len=sub_len, pitch_l=pitch_l,
                             pitch_z=pitch_z, pitch_u=pitch_u)
    return pl.pallas_call(
        kern,
        grid=(n,),
        in_specs=[
            row_spec(D_MODEL),
            pl.BlockSpec((HALO, D_MODEL), lambda i: (jnp.maximum(i * hb - 1, 0), 0)),
            pl.BlockSpec((HALO, D_MODEL), lambda i: (jnp.minimum((i + 1) * hb, n_hb - 1), 0)),
        ] + [_const_spec(c.shape) for c in consts],
        out_specs=[row_spec(D_LRU), row_spec(D_LRU), row_spec(D_LRU), row_spec(D_LRU),
                   row_spec(D_LRU), pl.BlockSpec((1, D_LRU), lambda i: (0, 0))],
        out_shape=[
            jax.ShapeDtypeStruct((rows, D_LRU), F32),
            jax.ShapeDtypeStruct((rows, D_LRU), F32),
            jax.ShapeDtypeStruct((rows, D_LRU), F32),
            jax.ShapeDtypeStruct((rows, D_LRU), F32),
            jax.ShapeDtypeStruct((rows, D_LRU), BF16),
            jax.ShapeDtypeStruct((1, D_LRU), F32),
        ],
        scratch_shapes=[
            pltpu.VMEM((ext, D_MODEL), BF16),
            pltpu.VMEM((ext, D_EXT), F32),
            pltpu.VMEM((D_LRU // LANES, SUBLANES * pitch_l, LANES), F32),
            pltpu.VMEM((tile, D_LRU), F32),
            pltpu.VMEM((tile, D_LRU), F32),
            pltpu.VMEM((tile, D_LRU), F32),
            pltpu.VMEM((D_CONV // LANES, SUBLANES * pitch_z, LANES), F32),
            pltpu.VMEM((CONV_WIDTH, SUBLANES, D_CONV), F32),
            pltpu.VMEM((tile, D_CONV), F32),
            pltpu.VMEM((D_CONV // LANES, SUBLANES * pitch_u, LANES), F32),
            pltpu.VMEM((1, D_LRU), F32),
        ],
        compiler_params=pltpu.CompilerParams(
            dimension_semantics=("arbitrary",), vmem_limit_bytes=VMEM_LIMIT),
        name="pass_a",
    )(x, x, x, *consts)


def _pass_b_call(a_out, x, gate_row, h0, w_out, final_g, final_norm):
    hf, ab, bb, gl, yg = a_out
    rows = x.shape[0]
    tile, n, sub_len = _tiling(rows)
    pitch_u = _odd_pitch(sub_len)
    row_spec = lambda w: pl.BlockSpec((tile, w), lambda i: (n - 1 - i, 0))
    consts = [w_out, gate_row, h0, final_g]
    kern = functools.partial(_pass_b_kernel, tile=tile, sub_len=sub_len, pitch_u=pitch_u,
                             final_norm=final_norm)
    return pl.pallas_call(
        kern,
        grid=(n,),
        in_specs=[row_spec(D_LRU)] * 5 + [row_spec(D_MODEL)] + [_const_spec(c.shape) for c in consts],
        out_specs=[row_spec(D_MODEL), pl.BlockSpec((1, D_LRU), lambda i: (0, 0))],
        out_shape=[jax.ShapeDtypeStruct((rows, D_MODEL), F32),
                   jax.ShapeDtypeStruct((1, D_LRU), F32)],
        scratch_shapes=[
            pltpu.VMEM((tile, D_LRU), F32),
            pltpu.VMEM((tile, D_LRU), F32),
            pltpu.VMEM((D_LRU // LANES, SUBLANES * pitch_u, LANES), F32),
            pltpu.VMEM((tile, D_LRU), BF16),
            pltpu.VMEM((1, D_LRU), F32),
        ],
        compiler_params=pltpu.CompilerParams(
            dimension_semantics=("arbitrary",), vmem_limit_bytes=VMEM_LIMIT),
        name="pass_b",
    )(hf, ab, bb, gl, yg, x, *consts)


def _layer_params(l, w_in, w_out, norm_g, lru_conv_w, lru_conv_b, lru_lam, lru_w_r, lru_b_r,
                  lru_w_i, lru_b_i, cv_w, cv_b, cv_ln_g, cv_ln_b, sgu_ln_g, sgu_ln_b, sgu_w, sgu_b):
    gates = jnp.stack([lru_w_r[l, 0], lru_w_i[l, 0], lru_w_r[l, 1], lru_w_i[l, 1]])
    hpb = GATE_BLK // LRU_HEAD_DIM
    nb = D_LRU // GATE_BLK
    gates = gates.reshape(4, nb, hpb, LRU_HEAD_DIM, LRU_HEAD_DIM)
    bd = jnp.einsum("tjqio,qr->tjqiro", gates, jnp.eye(hpb, dtype=gates.dtype))
    bd = bd.reshape(4, nb, GATE_BLK, GATE_BLK).transpose(1, 2, 0, 3).reshape(nb, GATE_BLK, 4 * GATE_BLK)
    b_gate = jnp.stack([lru_b_r[l, 0], lru_b_i[l, 0], lru_b_r[l, 1], lru_b_i[l, 1]]).reshape(4, D_LRU)
    return dict(
        norm_g=norm_g[l].reshape(1, D_MODEL),
        w_in=w_in[l].astype(BF16),
        w_out=w_out[l].astype(BF16),
        lru_conv_w=lru_conv_w[l],
        lru_conv_b=lru_conv_b[l].reshape(1, D_LRU),
        lru_lam=lru_lam[l],
        w_gate=bd.astype(BF16),
        b_gate=b_gate,
        cv_w=cv_w[l],
        cv_b=cv_b[l].reshape(1, D_CONV),
        cv_ln_g=cv_ln_g[l].reshape(1, D_CONV),
        cv_ln_b=cv_ln_b[l].reshape(1, D_CONV),
        sgu_ln_g=sgu_ln_g[l].reshape(1, D_SGU),
        sgu_ln_b=sgu_ln_b[l].reshape(1, D_SGU),
        sgu_w=sgu_w[l].astype(BF16),
        sgu_bias=jnp.repeat(sgu_b[l].T, D_SGU // SGU_HEADS, axis=1),
    )


def kernel(x, c, ctx, c_ctx, w_mod, b_mod, norm_g, w_in, w_out, lru_conv_w, lru_conv_b, lru_lam,
           lru_w_r, lru_b_r, lru_w_i, lru_b_i, cv_w, cv_b, cv_ln_g, cv_ln_b, sgu_ln_g, sgu_ln_b,
           sgu_w, sgu_b, final_g):
    assert x.shape[0] == 1 and c.shape[0] == 1 and ctx.shape[0] == 1
    depth = w_mod.shape[0]
    xs = x[0]
    cs = ctx[0]
    cvec = jnp.concatenate(
        [c, c_ctx.reshape(1, D_MODEL), jnp.zeros((SUBLANES - 2, D_MODEL), F32)], axis=0)
    mod = _mod_call(cvec, w_mod, b_mod)
    fg = final_g.reshape(1, D_MODEL)
    zero_state = jnp.zeros((1, D_LRU), F32)
    for l in range(depth):
        last = l == depth - 1
        p = _layer_params(l, w_in, w_out, norm_g, lru_conv_w, lru_conv_b, lru_lam, lru_w_r, lru_b_r,
                          lru_w_i, lru_b_i, cv_w, cv_b, cv_ln_g, cv_ln_b, sgu_ln_g, sgu_ln_b,
                          sgu_w, sgu_b)
        mod_x = mod[l, 0:1]
        mod_c = mod[l, 1:2]
        *a_ctx, hf_end = _pass_a_call(cs, mod_c[:, 0:2 * D_MODEL], zero_state, p)
        cs_new, hb_end = _pass_b_call(a_ctx, cs, mod_c[:, 2 * D_MODEL:], zero_state, p["w_out"], fg, False)
        *a_lat, _ = _pass_a_call(xs, mod_x[:, 0:2 * D_MODEL], hf_end, p)
        xs, _ = _pass_b_call(a_lat, xs, mod_x[:, 2 * D_MODEL:], hb_end, p["w_out"], fg, last)
        cs = cs_new
    return xs[None]
```

```python
import functools

import jax
import jax.numpy as jnp
from jax import lax
from jax.experimental import pallas as pl
from jax.experimental.pallas import tpu as pltpu

F32 = jnp.float32
BF16 = jnp.bfloat16

D_MODEL = 2048
EPS = 1e-6
D_LRU = 1024
LRU_HEADS = 16
LRU_HEAD_DIM = D_LRU // LRU_HEADS
LRU_C = 8.0
LRU_CONV = 4
D_CONV = 512
CONV_WIDTH = 31
D_SGU = 512
SGU_HEADS = 8
CHUNK = 128
D_IN = 5120
D_EXT = D_LRU + 2 * D_CONV
O_LRU_G = D_LRU
O_CV = 2 * D_LRU
O_CV_G = O_CV + 2 * D_CONV
O_SGU = O_CV_G + D_CONV
LOG2_E = 1.4426950408889634
M_HF, M_AB, M_BB, M_GL, D_MID = 0, D_LRU, 2 * D_LRU, 3 * D_LRU, 4 * D_LRU

LANES = 128
SUBLANES = 8
HALO = 16
LRU_HALO = 8
TILE = 256
TILE_B = 512
GATE_BLK = 256
VMEM_LIMIT = 56 * 1024 * 1024


def _sigmoid(v):
    return 1.0 / (1.0 + jnp.exp2(v * (-LOG2_E)))


def _silu(v):
    hv = 0.5 * v
    return hv + hv * jnp.tanh(hv)


def _odd_pitch(rows):
    return rows if (rows // SUBLANES) % 2 == 1 else rows + SUBLANES


def _replicate_subchunks(src_ref, col0, n_slab, dst_s, row0, rows, sub_len, pitch):
    for q in range(n_slab):
        cols = slice(col0 + q * LANES, col0 + (q + 1) * LANES)
        for s in range(SUBLANES):
            r = row0 + s * sub_len
            dst_s[q, s * pitch:s * pitch + rows, :] = src_ref[r:r + rows, cols]


def _interleaved_conv(src_s, n_slab, pitch, first, taps, weight, bias, out_s, sub_len):
    for q in range(n_slab):
        cols = slice(q * LANES, (q + 1) * LANES)
        acc = [bias(q)] * sub_len
        for m in range(sub_len + taps - 1):
            v = src_s[q, pl.ds(first + m, SUBLANES, stride=pitch), :]
            for j in range(sub_len):
                k = m - j
                if 0 <= k < taps:
                    acc[j] = acc[j] + weight(k, q) * v
        for j in range(sub_len):
            out_s[SUBLANES * j:SUBLANES * (j + 1), cols] = acc[j]


def _to_natural(val_fn, n_slab, u_s, sub_len, pitch, emit):
    for q in range(n_slab):
        for j in range(sub_len):
            u_s[q, pl.ds(j, SUBLANES, stride=pitch), :] = val_fn(j, q)
        for s in range(SUBLANES):
            emit(s, q, u_s[q, s * pitch:s * pitch + sub_len, :])


def _scan_loop(a_load, b_load, h_store, p_store, sub_len, reverse):
    n_slab = D_LRU // LANES

    hs = [jnp.zeros((SUBLANES, LANES), F32)] * n_slab
    ps = [jnp.ones((SUBLANES, LANES), F32)] * n_slab
    for jj in range(sub_len):
        r = SUBLANES * ((sub_len - 1 - jj) if reverse else jj)
        for q in range(n_slab):
            cols = slice(q * LANES, (q + 1) * LANES)
            a = a_load(r, cols)
            hs[q] = a * hs[q] + b_load(r, cols)
            ps[q] = a * ps[q]
            h_store(r, cols, hs[q])
            p_store(r, cols, ps[q])
    return jnp.concatenate(hs, axis=1), jnp.concatenate(ps, axis=1)


def _chain_states(h_end, p_end, carry_ref, reverse):
    c = carry_ref[...]
    rows = [None] * SUBLANES
    order = range(SUBLANES - 1, -1, -1) if reverse else range(SUBLANES)
    for s in order:
        rows[s] = c
        c = h_end[s:s + 1, :] + p_end[s:s + 1, :] * c
    carry_ref[...] = c
    return jnp.concatenate(rows, axis=0)


def _row_loader(ref, col0=0, row0=0):
    return lambda r, cols: ref[pl.ds(row0 + r, SUBLANES), slice(col0 + cols.start, col0 + cols.stop)]


def _row_storer(ref):
    def store(r, cols, v):
        ref[pl.ds(r, SUBLANES), cols] = v
    return store


def _mod_kernel(c_ref, w_ref, b_ref, o_ref):
    act = _silu(c_ref[...]).astype(BF16)
    o_ref[0] = jnp.dot(act, w_ref[0].astype(BF16), preferred_element_type=F32) + b_ref[0]


def _pass_a_kernel(x_ref, xp_ref, xn_ref, mod_ref, ng_ref, w_ref, lcw_ref, lcb_ref, lam_ref,
                   wg_ref, bg_ref, cvw_ref, cvb_ref, cvg_ref, cvbt_ref, sgg_ref, sgbt_ref,
                   sw_ref, sbias_ref, h0_ref,
                   mid_ref, yg_ref, hend_ref,
                   hn_s, pext_s, cl_s, xl_s, a_s, b_s, cz_s, w8_s, yc_s, u_s, carry_s,
                   *, tile, sub_len, pitch_l, pitch_z, pitch_u):
    i = pl.program_id(0)
    n = pl.num_programs(0)
    ext = tile + 2 * HALO
    w = w_ref.at[0]

    @pl.when(i == 0)
    def _():
        carry_s[...] = h0_ref[...]
        for k in range(CONV_WIDTH):
            w8_s[k] = jnp.broadcast_to(cvw_ref[k:k + 1, :], (SUBLANES, D_CONV))

    shift = mod_ref[:, 0:D_MODEL]
    gain = ng_ref[...] * (1.0 + mod_ref[:, D_MODEL:2 * D_MODEL])

    def norm_mod(v, keep=None):
        ms = jnp.mean(v * v, axis=-1, keepdims=True)
        h = v * lax.rsqrt(ms + EPS) * gain + shift
        if keep is not None:
            h = h * keep
        return h.astype(BF16)

    keep_prev = jnp.where(i == 0, 0.0, 1.0).astype(F32)
    keep_next = jnp.where(i == n - 1, 0.0, 1.0).astype(F32)
    hn_s[0:HALO, :] = norm_mod(xp_ref[...], keep_prev)
    hn_s[HALO:HALO + tile, :] = norm_mod(x_ref[...])
    hn_s[HALO + tile:ext, :] = norm_mod(xn_ref[...], keep_next)

    pext_s[:, 0:D_LRU] = jnp.dot(hn_s[...], w[:, 0:D_LRU], preferred_element_type=F32)
    pext_s[:, D_LRU:D_EXT] = jnp.dot(hn_s[...], w[:, O_CV:O_CV_G], preferred_element_type=F32)

    n_slab = D_LRU // LANES
    _replicate_subchunks(pext_s, 0, n_slab, cl_s, HALO - LRU_HALO, sub_len + 2 * LRU_HALO,
                         sub_len, pitch_l)
    lru_taps = {}

    def lru_weight(k, q):
        if (k, q) not in lru_taps:
            lru_taps[k, q] = jnp.broadcast_to(
                lcw_ref[k:k + 1, q * LANES:(q + 1) * LANES], (SUBLANES, LANES))
        return lru_taps[k, q]

    _interleaved_conv(
        cl_s, n_slab, pitch_l, LRU_HALO - LRU_CONV // 2, LRU_CONV, lru_weight,
        lambda q: jnp.broadcast_to(lcb_ref[:, q * LANES:(q + 1) * LANES], (SUBLANES, LANES)),
        xl_s, sub_len)

    neg_lam = -lam_ref[...]
    softplus = jnp.maximum(neg_lam, 0.0) + jnp.log1p(jnp.exp(-jnp.abs(neg_lam)))
    c_lam = -LRU_C * softplus
    for j in range(D_LRU // GATE_BLK):
        cs = slice(GATE_BLK * j, GATE_BLK * (j + 1))
        xl = xl_s[:, cs]
        g = jnp.dot(xl.astype(BF16), wg_ref[j], preferred_element_type=F32)
        for d in range(2):
            g0 = 2 * GATE_BLK * d
            r = _sigmoid(g[:, g0:g0 + GATE_BLK] + bg_ref[2 * d:2 * d + 1, cs])
            gi = _sigmoid(g[:, g0 + GATE_BLK:g0 + 2 * GATE_BLK] + bg_ref[2 * d + 1:2 * d + 2, cs])
            log_a = c_lam[d:d + 1, cs] * r
            a = jnp.exp(log_a)
            bv = jnp.sqrt(-jnp.tanh(log_a) * (a * a + 1.0)) * (gi * xl)
            if d == 0:
                a_s[:, cs] = a
                b_s[:, cs] = bv
            else:
                mid_ref[:, M_AB + GATE_BLK * j:M_AB + GATE_BLK * (j + 1)] = a
                mid_ref[:, M_BB + GATE_BLK * j:M_BB + GATE_BLK * (j + 1)] = bv

    h_end, p_end = _scan_loop(_row_loader(a_s), _row_loader(b_s), _row_storer(b_s), _row_storer(a_s),
                              sub_len, reverse=False)
    c_in = _chain_states(h_end, p_end, carry_s, reverse=False)
    hend_ref[...] = carry_s[...]
    for j in range(sub_len):
        rows = slice(SUBLANES * j, SUBLANES * (j + 1))
        mid_ref[rows, M_HF:M_HF + D_LRU] = b_s[rows, :] + a_s[rows, :] * c_in

    hn_main = hn_s.at[HALO:HALO + tile, :]
    mid_ref[:, M_GL:M_GL + D_LRU] = _silu(
        jnp.dot(hn_main[...], w[:, O_LRU_G:O_CV], preferred_element_type=F32))

    n_slab = D_CONV // LANES
    for q in range(n_slab):
        ca = slice(D_LRU + q * LANES, D_LRU + (q + 1) * LANES)
        cg = slice(D_LRU + D_CONV + q * LANES, D_LRU + D_CONV + (q + 1) * LANES)
        for s in range(SUBLANES):
            r = s * sub_len
            rows = sub_len + 2 * HALO
            cz_s[q, s * pitch_z:s * pitch_z + rows, :] = (
                pext_s[r:r + rows, ca] * _sigmoid(pext_s[r:r + rows, cg]))
    _interleaved_conv(
        cz_s, n_slab, pitch_z, HALO - CONV_WIDTH // 2, CONV_WIDTH,
        lambda k, q: w8_s[k, :, q * LANES:(q + 1) * LANES],
        lambda q: jnp.broadcast_to(cvb_ref[:, q * LANES:(q + 1) * LANES], (SUBLANES, LANES)),
        yc_s, sub_len)
    acc = yc_s[...]
    mu = jnp.mean(acc, axis=-1, keepdims=True)
    cen = acc - mu
    var = jnp.mean(cen * cen, axis=-1, keepdims=True)
    yc_s[...] = _silu(cen * lax.rsqrt(var + EPS) * cvg_ref[...] + cvbt_ref[...])
    g_cv = _silu(jnp.dot(hn_main[...], w[:, O_CV_G:O_SGU], preferred_element_type=F32))

    def emit_cv(s, q, block):
        rows = slice(s * sub_len, (s + 1) * sub_len)
        cols = slice(q * LANES, (q + 1) * LANES)
        yg_ref[rows, cols] = (block * g_cv[rows, cols]).astype(BF16)

    _to_natural(lambda j, q: yc_s[SUBLANES * j:SUBLANES * (j + 1), q * LANES:(q + 1) * LANES],
                n_slab, u_s, sub_len, pitch_u, emit_cv)

    p_sg = jnp.dot(hn_main[...], w[:, O_SGU:D_IN], preferred_element_type=F32)
    zz = jax.nn.gelu(p_sg[:, 0:2 * D_SGU])
    u = zz[:, 0:D_SGU]
    v = zz[:, D_SGU:2 * D_SGU]
    mu = jnp.mean(v, axis=-1, keepdims=True)
    cen = v - mu
    var = jnp.mean(cen * cen, axis=-1, keepdims=True)
    vn = (cen * lax.rsqrt(var + EPS) * sgg_ref[...] + sgbt_ref[...]).astype(BF16)
    n_ch = tile // CHUNK
    lane = lax.broadcasted_iota(jnp.int32, (CHUNK, n_ch * LANES), 1) % LANES
    even_head = lane < (D_SGU // SGU_HEADS)
    g_sg = _silu(p_sg[:, 2 * D_SGU:3 * D_SGU])
    for k in range(D_SGU // LANES):
        cols = slice(LANES * k, LANES * (k + 1))
        rhs = jnp.concatenate([vn[c * CHUNK:(c + 1) * CHUNK, cols] for c in range(n_ch)], axis=1)
        s_even = jnp.dot(sw_ref[2 * k], rhs, preferred_element_type=F32)
        s_odd = jnp.dot(sw_ref[2 * k + 1], rhs, preferred_element_type=F32)
        sel = jnp.where(even_head, s_even, s_odd)
        for c in range(n_ch):
            rows = slice(c * CHUNK, (c + 1) * CHUNK)
            s_c = sel[:, c * LANES:(c + 1) * LANES] + sbias_ref[:, cols]
            yg_ref[rows, D_CONV + LANES * k:D_CONV + LANES * (k + 1)] = (
                u[rows, cols] * s_c * g_sg[rows, cols]).astype(BF16)


def _pass_b_kernel(mid_ref, yg_ref, x_ref, wo_ref, gate_ref, h0_ref, fg_ref,
                   xo_ref, hend_ref, hb_s, p_s, u_s, yl_s, carry_s,
                   *, tile, sub_len, pitch_u, final_norm):
    i = pl.program_id(0)
    wo = wo_ref.at[0]

    @pl.when(i == 0)
    def _():
        carry_s[...] = h0_ref[...]

    unit = SUBLANES * sub_len
    for base in range(tile - unit, -1, -unit):
        h_end, p_end = _scan_loop(
            _row_loader(mid_ref, M_AB, base), _row_loader(mid_ref, M_BB, base),
            _row_storer(hb_s), _row_storer(p_s), sub_len, reverse=True)
        c_in = _chain_states(h_end, p_end, carry_s, reverse=True)

        def y_step(j, q, base=base, c_in=c_in):
            rows = slice(SUBLANES * j, SUBLANES * (j + 1))
            cols = slice(q * LANES, (q + 1) * LANES)
            hf = mid_ref[base + SUBLANES * j:base + SUBLANES * (j + 1),
                         M_HF + q * LANES:M_HF + (q + 1) * LANES]
            return hf + hb_s[rows, cols] + p_s[rows, cols] * c_in[:, cols]

        def emit(s, q, block, base=base):
            rows = slice(base + s * sub_len, base + (s + 1) * sub_len)
            gl = mid_ref[rows, M_GL + q * LANES:M_GL + (q + 1) * LANES]
            yl_s[rows, q * LANES:(q + 1) * LANES] = (block * gl).astype(BF16)

        _to_natural(y_step, D_LRU // LANES, u_s, sub_len, pitch_u, emit)
    hend_ref[...] = carry_s[...]

    o = jnp.dot(yl_s[...], wo[0:D_LRU, :], preferred_element_type=F32)
    o = o + jnp.dot(yg_ref[...], wo[D_LRU:D_MODEL, :], preferred_element_type=F32)
    xn = x_ref[...] + gate_ref[...] * o
    if final_norm:
        ms = jnp.mean(xn * xn, axis=-1, keepdims=True)
        xn = xn * lax.rsqrt(ms + EPS) * fg_ref[...]
    xo_ref[...] = xn


def _const_spec(shape):
    nd = len(shape)
    return pl.BlockSpec(shape, lambda i: (0,) * nd, pipeline_mode=pl.Buffered(1))


def _layer_spec(stacked, layer):
    shape = (1,) + stacked.shape[1:]
    nd = len(shape)
    return pl.BlockSpec(shape, lambda i: (layer,) + (0,) * (nd - 1), pipeline_mode=pl.Buffered(1))


def _mod_call(cvec, w_mod, b_mod):
    depth = w_mod.shape[0]
    bn = 768
    return pl.pallas_call(
        _mod_kernel,
        grid=(depth, 3 * D_MODEL // bn),
        in_specs=[
            pl.BlockSpec((SUBLANES, D_MODEL), lambda l, j: (0, 0)),
            pl.BlockSpec((1, D_MODEL, bn), lambda l, j: (l, 0, j)),
            pl.BlockSpec((1, 1, bn), lambda l, j: (l, 0, j)),
        ],
        out_specs=pl.BlockSpec((1, SUBLANES, bn), lambda l, j: (l, 0, j)),
        out_shape=jax.ShapeDtypeStruct((depth, SUBLANES, 3 * D_MODEL), F32),
        compiler_params=pltpu.CompilerParams(dimension_semantics=("arbitrary", "arbitrary")),
        name="mod_proj",
    )(cvec, w_mod, b_mod.reshape(depth, 1, 3 * D_MODEL))


def _tiling(rows):
    tile = min(TILE, rows)
    assert rows % tile == 0 and tile % (SUBLANES * SUBLANES) == 0 and tile % CHUNK == 0
    sub_len = tile // SUBLANES
    return tile, rows // tile, sub_len


def _pass_a_call(x, mod_row, h0, layer, w_in, p):
    rows = x.shape[0]
    tile, n, sub_len = _tiling(rows)
    pitch_l = _odd_pitch(sub_len + 2 * LRU_HALO)
    pitch_z = _odd_pitch(sub_len + 2 * HALO)
    pitch_u = _odd_pitch(sub_len)
    ext = tile + 2 * HALO
    hb = tile // HALO
    n_hb = rows // HALO
    row_spec = lambda w: pl.BlockSpec((tile, w), lambda i: (i, 0))
    consts = [p["lru_conv_w"], p["lru_conv_b"], p["lru_lam"],
              p["w_gate"], p["b_gate"], p["cv_w"], p["cv_b"], p["cv_ln_g"], p["cv_ln_b"],
              p["sgu_ln_g"], p["sgu_ln_b"], p["sgu_w"], p["sgu_bias"], h0]
    kern = functools.partial(_pass_a_kernel, tile=tile, sub_len=sub_len, pitch_l=pitch_l,
                             pitch_z=pitch_z, pitch_u=pitch_u)
    return pl.pallas_call(
        kern,
        grid=(n,),
        in_specs=[
            row_spec(D_MODEL),
            pl.BlockSpec((HALO, D_MODEL), lambda i: (jnp.maximum(i * hb - 1, 0), 0)),
            pl.BlockSpec((HALO, D_MODEL), lambda i: (jnp.minimum((i + 1) * hb, n_hb - 1), 0)),
            _const_spec(mod_row.shape), _const_spec(p["norm_g"].shape), _layer_spec(w_in, layer),
        ] + [_const_spec(c.shape) for c in consts],
        out_specs=[row_spec(D_MID), row_spec(D_LRU), pl.BlockSpec((1, D_LRU), lambda i: (0, 0))],
        out_shape=[
            jax.ShapeDtypeStruct((rows, D_MID), F32),
            jax.ShapeDtypeStruct((rows, D_LRU), BF16),
            jax.ShapeDtypeStruct((1, D_LRU), F32),
        ],
        scratch_shapes=[
            pltpu.VMEM((ext, D_MODEL), BF16),
            pltpu.VMEM((ext, D_EXT), F32),
            pltpu.VMEM((D_LRU // LANES, SUBLANES * pitch_l, LANES), F32),
            pltpu.VMEM((tile, D_LRU), F32),
            pltpu.VMEM((tile, D_LRU), F32),
            pltpu.VMEM((tile, D_LRU), F32),
            pltpu.VMEM((D_CONV // LANES, SUBLANES * pitch_z, LANES), F32),
            pltpu.VMEM((CONV_WIDTH, SUBLANES, D_CONV), F32),
            pltpu.VMEM((tile, D_CONV), F32),
            pltpu.VMEM((D_CONV // LANES, SUBLANES * pitch_u, LANES), F32),
            pltpu.VMEM((1, D_LRU), F32),
        ],
        compiler_params=pltpu.CompilerParams(
            dimension_semantics=("arbitrary",), vmem_limit_bytes=VMEM_LIMIT),
        name="pass_a",
    )(x, x, x, mod_row, p["norm_g"], w_in, *consts)


def _pass_b_call(mid, yg, x, gate_row, h0, layer, w_out, final_g, final_norm):
    rows = x.shape[0]
    unit, _, sub_len = _tiling(rows)
    tile = min(TILE_B, rows)
    assert rows % tile == 0 and tile % unit == 0
    n = rows // tile
    pitch_u = _odd_pitch(sub_len)
    row_spec = lambda w: pl.BlockSpec((tile, w), lambda i: (n - 1 - i, 0))
    consts = [gate_row, h0, final_g]
    kern = functools.partial(_pass_b_kernel, tile=tile, sub_len=sub_len, pitch_u=pitch_u,
                             final_norm=final_norm)
    return pl.pallas_call(
        kern,
        grid=(n,),
        in_specs=[row_spec(D_MID), row_spec(D_LRU), row_spec(D_MODEL), _layer_spec(w_out, layer)]
        + [_const_spec(c.shape) for c in consts],
        out_specs=[row_spec(D_MODEL), pl.BlockSpec((1, D_LRU), lambda i: (0, 0))],
        out_shape=[jax.ShapeDtypeStruct((rows, D_MODEL), F32),
                   jax.ShapeDtypeStruct((1, D_LRU), F32)],
        scratch_shapes=[
            pltpu.VMEM((unit, D_LRU), F32),
            pltpu.VMEM((unit, D_LRU), F32),
            pltpu.VMEM((D_LRU // LANES, SUBLANES * pitch_u, LANES), F32),
            pltpu.VMEM((tile, D_LRU), BF16),
            pltpu.VMEM((1, D_LRU), F32),
        ],
        compiler_params=pltpu.CompilerParams(
            dimension_semantics=("arbitrary",), vmem_limit_bytes=VMEM_LIMIT),
        name="pass_b",
    )(mid, yg, x, w_out, *consts)


def _layer_params(l, norm_g, lru_conv_w, lru_conv_b, lru_lam, lru_w_r, lru_b_r, lru_w_i, lru_b_i,
                  cv_w, cv_b, cv_ln_g, cv_ln_b, sgu_ln_g, sgu_ln_b, sgu_w, sgu_b):
    gates = jnp.stack([lru_w_r[l, 0], lru_w_i[l, 0], lru_w_r[l, 1], lru_w_i[l, 1]])
    hpb = GATE_BLK // LRU_HEAD_DIM
    nb = D_LRU // GATE_BLK
    gates = gates.reshape(4, nb, hpb, LRU_HEAD_DIM, LRU_HEAD_DIM)
    bd = jnp.einsum("tjqio,qr->tjqiro", gates, jnp.eye(hpb, dtype=gates.dtype))
    bd = bd.reshape(4, nb, GATE_BLK, GATE_BLK).transpose(1, 2, 0, 3).reshape(nb, GATE_BLK, 4 * GATE_BLK)
    b_gate = jnp.stack([lru_b_r[l, 0], lru_b_i[l, 0], lru_b_r[l, 1], lru_b_i[l, 1]]).reshape(4, D_LRU)
    return dict(
        norm_g=norm_g[l].reshape(1, D_MODEL),
        lru_conv_w=lru_conv_w[l],
        lru_conv_b=lru_conv_b[l].reshape(1, D_LRU),
        lru_lam=lru_lam[l],
        w_gate=bd.astype(BF16),
        b_gate=b_gate,
        cv_w=cv_w[l],
        cv_b=cv_b[l].reshape(1, D_CONV),
        cv_ln_g=cv_ln_g[l].reshape(1, D_CONV),
        cv_ln_b=cv_ln_b[l].reshape(1, D_CONV),
        sgu_ln_g=sgu_ln_g[l].reshape(1, D_SGU),
        sgu_ln_b=sgu_ln_b[l].reshape(1, D_SGU),
        sgu_w=sgu_w[l].astype(BF16),
        sgu_bias=jnp.repeat(sgu_b[l].T, D_SGU // SGU_HEADS, axis=1),
    )


def kernel(x, c, ctx, c_ctx, w_mod, b_mod, norm_g, w_in, w_out, lru_conv_w, lru_conv_b, lru_lam,
           lru_w_r, lru_b_r, lru_w_i, lru_b_i, cv_w, cv_b, cv_ln_g, cv_ln_b, sgu_ln_g, sgu_ln_b,
           sgu_w, sgu_b, final_g):
    assert x.shape[0] == 1 and c.shape[0] == 1 and ctx.shape[0] == 1
    depth = w_mod.shape[0]
    xs = x[0]
    cs = ctx[0]
    cvec = jnp.concatenate(
        [c, c_ctx.reshape(1, D_MODEL), jnp.zeros((SUBLANES - 2, D_MODEL), F32)], axis=0)
    mod = _mod_call(cvec, w_mod, b_mod)
    fg = final_g.reshape(1, D_MODEL)
    zero_state = jnp.zeros((1, D_LRU), F32)
    w_in_bf = w_in.astype(BF16)
    w_out_bf = w_out.astype(BF16)
    for l in range(depth):
        last = l == depth - 1
        p = _layer_params(l, norm_g, lru_conv_w, lru_conv_b, lru_lam, lru_w_r, lru_b_r, lru_w_i,
                          lru_b_i, cv_w, cv_b, cv_ln_g, cv_ln_b, sgu_ln_g, sgu_ln_b, sgu_w, sgu_b)
        mod_x = mod[l, 0:1]
        mod_c = mod[l, 1:2]
        mid_c, yg_c, hf_end = _pass_a_call(cs, mod_c[:, 0:2 * D_MODEL], zero_state, l, w_in_bf, p)
        cs_new, hb_end = _pass_b_call(mid_c, yg_c, cs, mod_c[:, 2 * D_MODEL:], zero_state, l, w_out_bf,
                                      fg, False)
        mid_x, yg_x, _ = _pass_a_call(xs, mod_x[:, 0:2 * D_MODEL], hf_end, l, w_in_bf, p)
        xs, _ = _pass_b_call(mid_x, yg_x, xs, mod_x[:, 2 * D_MODEL:], hb_end, l, w_out_bf, fg, last)
        cs = cs_new
    return xs[None]
```

```python
import functools

import jax
import jax.numpy as jnp
from jax import lax
from jax.experimental import pallas as pl
from jax.experimental.pallas import tpu as pltpu

F32 = jnp.float32
BF16 = jnp.bfloat16

D_MODEL = 2048
EPS = 1e-6
D_LRU = 1024
LRU_HEADS = 16
LRU_HEAD_DIM = D_LRU // LRU_HEADS
LRU_C = 8.0
LRU_CONV = 4
D_CONV = 512
CONV_WIDTH = 31
D_SGU = 512
SGU_HEADS = 8
CHUNK = 128
D_IN = 5120
D_EXT = D_LRU + 2 * D_CONV
O_LRU_G = D_LRU
O_CV = 2 * D_LRU
O_CV_G = O_CV + 2 * D_CONV
O_SGU = O_CV_G + D_CONV
LOG2_E = 1.4426950408889634
M_HF, M_AB, M_BB, M_GL, D_MID = 0, D_LRU, 2 * D_LRU, 3 * D_LRU, 4 * D_LRU

LANES = 128
SUBLANES = 8
HALO = 16
LRU_HALO = 8
TILE = 256
TILE_B = 512
GATE_BLK = 256
VMEM_LIMIT = 56 * 1024 * 1024


def _sigmoid(v):
    return 1.0 / (1.0 + jnp.exp2(v * (-LOG2_E)))


def _silu(v):
    hv = 0.5 * v
    return hv + hv * jnp.tanh(hv)


def _odd_pitch(rows):
    return rows if (rows // SUBLANES) % 2 == 1 else rows + SUBLANES


def _replicate_subchunks(src_ref, col0, n_slab, dst_s, row0, rows, sub_len, pitch):
    for q in range(n_slab):
        cols = slice(col0 + q * LANES, col0 + (q + 1) * LANES)
        for s in range(SUBLANES):
            r = row0 + s * sub_len
            dst_s[q, s * pitch:s * pitch + rows, :] = src_ref[r:r + rows, cols]


def _interleaved_conv(src_s, n_slab, pitch, first, taps, weight, bias, out_s, sub_len):
    for q in range(n_slab):
        cols = slice(q * LANES, (q + 1) * LANES)
        acc = [bias(q)] * sub_len
        for m in range(sub_len + taps - 1):
            v = src_s[q, pl.ds(first + m, SUBLANES, stride=pitch), :]
            for j in range(sub_len):
                k = m - j
                if 0 <= k < taps:
                    acc[j] = acc[j] + weight(k, q) * v
        for j in range(sub_len):
            out_s[SUBLANES * j:SUBLANES * (j + 1), cols] = acc[j]


def _to_natural(val_fn, n_slab, u_s, sub_len, pitch, emit):
    for q in range(n_slab):
        for j in range(sub_len):
            u_s[q, pl.ds(j, SUBLANES, stride=pitch), :] = val_fn(j, q)
        for s in range(SUBLANES):
            emit(s, q, u_s[q, s * pitch:s * pitch + sub_len, :])


def _scan_loop(a_load, b_load, h_store, p_store, sub_len, reverse):
    n_slab = D_LRU // LANES

    hs = [jnp.zeros((SUBLANES, LANES), F32)] * n_slab
    ps = [jnp.ones((SUBLANES, LANES), F32)] * n_slab
    for jj in range(sub_len):
        r = SUBLANES * ((sub_len - 1 - jj) if reverse else jj)
        for q in range(n_slab):
            cols = slice(q * LANES, (q + 1) * LANES)
            a = a_load(r, cols)
            hs[q] = a * hs[q] + b_load(r, cols)
            ps[q] = a * ps[q]
            h_store(r, cols, hs[q])
            p_store(r, cols, ps[q])
    return jnp.concatenate(hs, axis=1), jnp.concatenate(ps, axis=1)


def _chain_states(h_end, p_end, carry_ref, reverse):
    c = carry_ref[...]
    rows = [None] * SUBLANES
    order = range(SUBLANES - 1, -1, -1) if reverse else range(SUBLANES)
    for s in order:
        rows[s] = c
        c = h_end[s:s + 1, :] + p_end[s:s + 1, :] * c
    carry_ref[...] = c
    return jnp.concatenate(rows, axis=0)


def _row_loader(ref, col0=0, row0=0):
    return lambda r, cols: ref[pl.ds(row0 + r, SUBLANES), slice(col0 + cols.start, col0 + cols.stop)]


def _row_storer(ref):
    def store(r, cols, v):
        ref[pl.ds(r, SUBLANES), cols] = v
    return store


def _mod_kernel(c_ref, w_ref, b_ref, o_ref):
    act = _silu(c_ref[...]).astype(BF16)
    o_ref[0] = jnp.dot(act, w_ref[0].astype(BF16), preferred_element_type=F32) + b_ref[0]


def _pass_a_kernel(x_ref, xp_ref, xn_ref, mod_ref, ng_ref, w_ref, lcw_ref, lcb_ref, lam_ref,
                   wg_ref, bg_ref, cvw_ref, cvb_ref, cvg_ref, cvbt_ref, sgg_ref, sgbt_ref,
                   sw_ref, sbias_ref, h0_ref,
                   mid_ref, yg_ref, hend_ref,
                   hn_s, pext_s, cl_s, xl_s, a_s, b_s, cz_s, w8_s, yc_s, u_s, carry_s,
                   *, tile, sub_len, pitch_l, pitch_z, pitch_u):
    i = pl.program_id(0)
    n = pl.num_programs(0)
    ext = tile + 2 * HALO
    w = w_ref.at[0]

    @pl.when(i == 0)
    def _():
        carry_s[...] = h0_ref[...]
        for k in range(CONV_WIDTH):
            w8_s[k] = jnp.broadcast_to(cvw_ref[k:k + 1, :], (SUBLANES, D_CONV))

    shift = mod_ref[:, 0:D_MODEL]
    gain = ng_ref[...] * (1.0 + mod_ref[:, D_MODEL:2 * D_MODEL])

    def norm_mod(v, keep=None):
        ms = jnp.mean(v * v, axis=-1, keepdims=True)
        h = v * lax.rsqrt(ms + EPS) * gain + shift
        if keep is not None:
            h = h * keep
        return h.astype(BF16)

    keep_prev = jnp.where(i == 0, 0.0, 1.0).astype(F32)
    keep_next = jnp.where(i == n - 1, 0.0, 1.0).astype(F32)
    hn_s[0:HALO, :] = norm_mod(xp_ref[...], keep_prev)
    hn_s[HALO:HALO + tile, :] = norm_mod(x_ref[...])
    hn_s[HALO + tile:ext, :] = norm_mod(xn_ref[...], keep_next)

    pext_s[:, 0:D_LRU] = jnp.dot(hn_s[...], w[:, 0:D_LRU], preferred_element_type=F32)
    pext_s[:, D_LRU:D_EXT] = jnp.dot(hn_s[...], w[:, O_CV:O_CV_G], preferred_element_type=F32)

    n_slab = D_LRU // LANES
    _replicate_subchunks(pext_s, 0, n_slab, cl_s, HALO - LRU_HALO, sub_len + 2 * LRU_HALO,
                         sub_len, pitch_l)
    lru_taps = {}

    def lru_weight(k, q):
        if (k, q) not in lru_taps:
            lru_taps[k, q] = jnp.broadcast_to(
                lcw_ref[k:k + 1, q * LANES:(q + 1) * LANES], (SUBLANES, LANES))
        return lru_taps[k, q]

    _interleaved_conv(
        cl_s, n_slab, pitch_l, LRU_HALO - LRU_CONV // 2, LRU_CONV, lru_weight,
        lambda q: jnp.broadcast_to(lcb_ref[:, q * LANES:(q + 1) * LANES], (SUBLANES, LANES)),
        xl_s, sub_len)

    neg_lam = -lam_ref[...]
    softplus = jnp.maximum(neg_lam, 0.0) + jnp.log1p(jnp.exp(-jnp.abs(neg_lam)))
    decay = LRU_C * softplus
    decay_log2 = decay * (-LOG2_E)
    for j in range(D_LRU // GATE_BLK):
        cs = slice(GATE_BLK * j, GATE_BLK * (j + 1))
        xl = xl_s[:, cs]
        g = jnp.dot(xl.astype(BF16), wg_ref[j], preferred_element_type=F32)
        for d in range(2):
            g0 = 2 * GATE_BLK * d
            r = _sigmoid(g[:, g0:g0 + GATE_BLK] + bg_ref[2 * d:2 * d + 1, cs])
            gi = _sigmoid(g[:, g0 + GATE_BLK:g0 + 2 * GATE_BLK] + bg_ref[2 * d + 1:2 * d + 2, cs])
            neg_log_a = decay[d:d + 1, cs] * r
            a = jnp.exp2(decay_log2[d:d + 1, cs] * r)
            sq = jnp.tanh(neg_log_a) * (a * a + 1.0)
            root = jnp.where(sq > 0.0, sq * lax.rsqrt(sq), 0.0)
            bv = root * (gi * xl)
            if d == 0:
                a_s[:, cs] = a
                b_s[:, cs] = bv
            else:
                mid_ref[:, M_AB + GATE_BLK * j:M_AB + GATE_BLK * (j + 1)] = a
                mid_ref[:, M_BB + GATE_BLK * j:M_BB + GATE_BLK * (j + 1)] = bv

    h_end, p_end = _scan_loop(_row_loader(a_s), _row_loader(b_s), _row_storer(b_s), _row_storer(a_s),
                              sub_len, reverse=False)
    c_in = _chain_states(h_end, p_end, carry_s, reverse=False)
    hend_ref[...] = carry_s[...]
    for j in range(sub_len):
        rows = slice(SUBLANES * j, SUBLANES * (j + 1))
        mid_ref[rows, M_HF:M_HF + D_LRU] = b_s[rows, :] + a_s[rows, :] * c_in

    hn_main = hn_s.at[HALO:HALO + tile, :]
    mid_ref[:, M_GL:M_GL + D_LRU] = _silu(
        jnp.dot(hn_main[...], w[:, O_LRU_G:O_CV], preferred_element_type=F32))

    n_slab = D_CONV // LANES
    for q in range(n_slab):
        ca = slice(D_LRU + q * LANES, D_LRU + (q + 1) * LANES)
        cg = slice(D_LRU + D_CONV + q * LANES, D_LRU + D_CONV + (q + 1) * LANES)
        for s in range(SUBLANES):
            r = s * sub_len
            rows = sub_len + 2 * HALO
            cz_s[q, s * pitch_z:s * pitch_z + rows, :] = (
                pext_s[r:r + rows, ca] * _sigmoid(pext_s[r:r + rows, cg]))
    _interleaved_conv(
        cz_s, n_slab, pitch_z, HALO - CONV_WIDTH // 2, CONV_WIDTH,
        lambda k, q: w8_s[k, :, q * LANES:(q + 1) * LANES],
        lambda q: jnp.broadcast_to(cvb_ref[:, q * LANES:(q + 1) * LANES], (SUBLANES, LANES)),
        yc_s, sub_len)
    acc = yc_s[...]
    mu = jnp.mean(acc, axis=-1, keepdims=True)
    cen = acc - mu
    var = jnp.mean(cen * cen, axis=-1, keepdims=True)
    yc_s[...] = _silu(cen * lax.rsqrt(var + EPS) * cvg_ref[...] + cvbt_ref[...])
    g_cv = _silu(jnp.dot(hn_main[...], w[:, O_CV_G:O_SGU], preferred_element_type=F32))

    def emit_cv(s, q, block):
        rows = slice(s * sub_len, (s + 1) * sub_len)
        cols = slice(q * LANES, (q + 1) * LANES)
        yg_ref[rows, cols] = (block * g_cv[rows, cols]).astype(BF16)

    _to_natural(lambda j, q: yc_s[SUBLANES * j:SUBLANES * (j + 1), q * LANES:(q + 1) * LANES],
                n_slab, u_s, sub_len, pitch_u, emit_cv)

    p_sg = jnp.dot(hn_main[...], w[:, O_SGU:D_IN], preferred_element_type=F32)
    zz = jax.nn.gelu(p_sg[:, 0:2 * D_SGU])
    u = zz[:, 0:D_SGU]
    v = zz[:, D_SGU:2 * D_SGU]
    mu = jnp.mean(v, axis=-1, keepdims=True)
    cen = v - mu
    var = jnp.mean(cen * cen, axis=-1, keepdims=True)
    vn = (cen * lax.rsqrt(var + EPS) * sgg_ref[...] + sgbt_ref[...]).astype(BF16)
    n_ch = tile // CHUNK
    lane = lax.broadcasted_iota(jnp.int32, (CHUNK, n_ch * LANES), 1) % LANES
    even_head = lane < (D_SGU // SGU_HEADS)
    g_sg = _silu(p_sg[:, 2 * D_SGU:3 * D_SGU])
    for k in range(D_SGU // LANES):
        cols = slice(LANES * k, LANES * (k + 1))
        rhs = jnp.concatenate([vn[c * CHUNK:(c + 1) * CHUNK, cols] for c in range(n_ch)], axis=1)
        s_even = jnp.dot(sw_ref[2 * k], rhs, preferred_element_type=F32)
        s_odd = jnp.dot(sw_ref[2 * k + 1], rhs, preferred_element_type=F32)
        sel = jnp.where(even_head, s_even, s_odd)
        for c in range(n_ch):
            rows = slice(c * CHUNK, (c + 1) * CHUNK)
            s_c = sel[:, c * LANES:(c + 1) * LANES] + sbias_ref[:, cols]
            yg_ref[rows, D_CONV + LANES * k:D_CONV + LANES * (k + 1)] = (
                u[rows, cols] * s_c * g_sg[rows, cols]).astype(BF16)


def _pass_b_kernel(mid_ref, yg_ref, x_ref, wo_ref, gate_ref, h0_ref, fg_ref,
                   xo_ref, hend_ref, hb_s, p_s, u_s, yl_s, carry_s,
                   *, tile, sub_len, pitch_u, final_norm):
    i = pl.program_id(0)
    wo = wo_ref.at[0]

    @pl.when(i == 0)
    def _():
        carry_s[...] = h0_ref[...]

    unit = SUBLANES * sub_len
    for base in range(tile - unit, -1, -unit):
        h_end, p_end = _scan_loop(
            _row_loader(mid_ref, M_AB, base), _row_loader(mid_ref, M_BB, base),
            _row_storer(hb_s), _row_storer(p_s), sub_len, reverse=True)
        c_in = _chain_states(h_end, p_end, carry_s, reverse=True)

        def y_step(j, q, base=base, c_in=c_in):
            rows = slice(SUBLANES * j, SUBLANES * (j + 1))
            cols = slice(q * LANES, (q + 1) * LANES)
            hf = mid_ref[base + SUBLANES * j:base + SUBLANES * (j + 1),
                         M_HF + q * LANES:M_HF + (q + 1) * LANES]
            return hf + hb_s[rows, cols] + p_s[rows, cols] * c_in[:, cols]

        def emit(s, q, block, base=base):
            rows = slice(base + s * sub_len, base + (s + 1) * sub_len)
            gl = mid_ref[rows, M_GL + q * LANES:M_GL + (q + 1) * LANES]
            yl_s[rows, q * LANES:(q + 1) * LANES] = (block * gl).astype(BF16)

        _to_natural(y_step, D_LRU // LANES, u_s, sub_len, pitch_u, emit)
    hend_ref[...] = carry_s[...]

    o = jnp.dot(yl_s[...], wo[0:D_LRU, :], preferred_element_type=F32)
    o = o + jnp.dot(yg_ref[...], wo[D_LRU:D_MODEL, :], preferred_element_type=F32)
    xn = x_ref[...] + gate_ref[...] * o
    if final_norm:
        ms = jnp.mean(xn * xn, axis=-1, keepdims=True)
        xn = xn * lax.rsqrt(ms + EPS) * fg_ref[...]
    xo_ref[...] = xn


def _const_spec(shape):
    nd = len(shape)
    return pl.BlockSpec(shape, lambda i: (0,) * nd, pipeline_mode=pl.Buffered(1))


def _layer_spec(stacked, layer):
    shape = (1,) + stacked.shape[1:]
    nd = len(shape)
    return pl.BlockSpec(shape, lambda i: (layer,) + (0,) * (nd - 1), pipeline_mode=pl.Buffered(1))


def _mod_call(cvec, w_mod, b_mod):
    depth = w_mod.shape[0]
    bn = 768
    return pl.pallas_call(
        _mod_kernel,
        grid=(depth, 3 * D_MODEL // bn),
        in_specs=[
            pl.BlockSpec((SUBLANES, D_MODEL), lambda l, j: (0, 0)),
            pl.BlockSpec((1, D_MODEL, bn), lambda l, j: (l, 0, j)),
            pl.BlockSpec((1, 1, bn), lambda l, j: (l, 0, j)),
        ],
        out_specs=pl.BlockSpec((1, SUBLANES, bn), lambda l, j: (l, 0, j)),
        out_shape=jax.ShapeDtypeStruct((depth, SUBLANES, 3 * D_MODEL), F32),
        compiler_params=pltpu.CompilerParams(dimension_semantics=("arbitrary", "arbitrary")),
        name="mod_proj",
    )(cvec, w_mod, b_mod.reshape(depth, 1, 3 * D_MODEL))


def _tiling(rows):
    tile = min(TILE, rows)
    assert rows % tile == 0 and tile % (SUBLANES * SUBLANES) == 0 and tile % CHUNK == 0
    sub_len = tile // SUBLANES
    return tile, rows // tile, sub_len


def _pass_a_call(x, mod_row, h0, layer, w_in, p):
    rows = x.shape[0]
    tile, n, sub_len = _tiling(rows)
    pitch_l = _odd_pitch(sub_len + 2 * LRU_HALO)
    pitch_z = _odd_pitch(sub_len + 2 * HALO)
    pitch_u = _odd_pitch(sub_len)
    ext = tile + 2 * HALO
    hb = tile // HALO
    n_hb = rows // HALO
    row_spec = lambda w: pl.BlockSpec((tile, w), lambda i: (i, 0))
    consts = [p["lru_conv_w"], p["lru_conv_b"], p["lru_lam"],
              p["w_gate"], p["b_gate"], p["cv_w"], p["cv_b"], p["cv_ln_g"], p["cv_ln_b"],
              p["sgu_ln_g"], p["sgu_ln_b"], p["sgu_w"], p["sgu_bias"], h0]
    kern = functools.partial(_pass_a_kernel, tile=tile, sub_len=sub_len, pitch_l=pitch_l,
                             pitch_z=pitch_z, pitch_u=pitch_u)
    return pl.pallas_call(
        kern,
        grid=(n,),
        in_specs=[
            row_spec(D_MODEL),
            pl.BlockSpec((HALO, D_MODEL), lambda i: (jnp.maximum(i * hb - 1, 0), 0)),
            pl.BlockSpec((HALO, D_MODEL), lambda i: (jnp.minimum((i + 1) * hb, n_hb - 1), 0)),
            _const_spec(mod_row.shape), _const_spec(p["norm_g"].shape), _layer_spec(w_in, layer),
        ] + [_const_spec(c.shape) for c in consts],
        out_specs=[row_spec(D_MID), row_spec(D_LRU), pl.BlockSpec((1, D_LRU), lambda i: (0, 0))],
        out_shape=[
            jax.ShapeDtypeStruct((rows, D_MID), F32),
            jax.ShapeDtypeStruct((rows, D_LRU), BF16),
            jax.ShapeDtypeStruct((1, D_LRU), F32),
        ],
        scratch_shapes=[
            pltpu.VMEM((ext, D_MODEL), BF16),
            pltpu.VMEM((ext, D_EXT), F32),
            pltpu.VMEM((D_LRU // LANES, SUBLANES * pitch_l, LANES), F32),
            pltpu.VMEM((tile, D_LRU), F32),
            pltpu.VMEM((tile, D_LRU), F32),
            pltpu.VMEM((tile, D_LRU), F32),
            pltpu.VMEM((D_CONV // LANES, SUBLANES * pitch_z, LANES), F32),
            pltpu.VMEM((CONV_WIDTH, SUBLANES, D_CONV), F32),
            pltpu.VMEM((tile, D_CONV), F32),
            pltpu.VMEM((D_CONV // LANES, SUBLANES * pitch_u, LANES), F32),
            pltpu.VMEM((1, D_LRU), F32),
        ],
        compiler_params=pltpu.CompilerParams(
            dimension_semantics=("arbitrary",), vmem_limit_bytes=VMEM_LIMIT),
        name="pass_a",
    )(x, x, x, mod_row, p["norm_g"], w_in, *consts)


def _pass_b_call(mid, yg, x, gate_row, h0, layer, w_out, final_g, final_norm):
    rows = x.shape[0]
    unit, _, sub_len = _tiling(rows)
    tile = min(TILE_B, rows)
    assert rows % tile == 0 and tile % unit == 0
    n = rows // tile
    pitch_u = _odd_pitch(sub_len)
    row_spec = lambda w: pl.BlockSpec((tile, w), lambda i: (n - 1 - i, 0))
    consts = [gate_row, h0, final_g]
    kern = functools.partial(_pass_b_kernel, tile=tile, sub_len=sub_len, pitch_u=pitch_u,
                             final_norm=final_norm)
    return pl.pallas_call(
        kern,
        grid=(n,),
        in_specs=[row_spec(D_MID), row_spec(D_LRU), row_spec(D_MODEL), _layer_spec(w_out, layer)]
        + [_const_spec(c.shape) for c in consts],
        out_specs=[row_spec(D_MODEL), pl.BlockSpec((1, D_LRU), lambda i: (0, 0))],
        out_shape=[jax.ShapeDtypeStruct((rows, D_MODEL), F32),
                   jax.ShapeDtypeStruct((1, D_LRU), F32)],
        scratch_shapes=[
            pltpu.VMEM((unit, D_LRU), F32),
            pltpu.VMEM((unit, D_LRU), F32),
            pltpu.VMEM((D_LRU // LANES, SUBLANES * pitch_u, LANES), F32),
            pltpu.VMEM((tile, D_LRU), BF16),
            pltpu.VMEM((1, D_LRU), F32),
        ],
        compiler_params=pltpu.CompilerParams(
            dimension_semantics=("arbitrary",), vmem_limit_bytes=VMEM_LIMIT),
        name="pass_b",
    )(mid, yg, x, w_out, *consts)


def _layer_params(l, norm_g, lru_conv_w, lru_conv_b, lru_lam, lru_w_r, lru_b_r, lru_w_i, lru_b_i,
                  cv_w, cv_b, cv_ln_g, cv_ln_b, sgu_ln_g, sgu_ln_b, sgu_w, sgu_b):
    gates = jnp.stack([lru_w_r[l, 0], lru_w_i[l, 0], lru_w_r[l, 1], lru_w_i[l, 1]])
    hpb = GATE_BLK // LRU_HEAD_DIM
    nb = D_LRU // GATE_BLK
    gates = gates.reshape(4, nb, hpb, LRU_HEAD_DIM, LRU_HEAD_DIM)
    bd = jnp.einsum("tjqio,qr->tjqiro", gates, jnp.eye(hpb, dtype=gates.dtype))
    bd = bd.reshape(4, nb, GATE_BLK, GATE_BLK).transpose(1, 2, 0, 3).reshape(nb, GATE_BLK, 4 * GATE_BLK)
    b_gate = jnp.stack([lru_b_r[l, 0], lru_b_i[l, 0], lru_b_r[l, 1], lru_b_i[l, 1]]).reshape(4, D_LRU)
    return dict(
        norm_g=norm_g[l].reshape(1, D_MODEL),
        lru_conv_w=lru_conv_w[l],
        lru_conv_b=lru_conv_b[l].reshape(1, D_LRU),
        lru_lam=lru_lam[l],
        w_gate=bd.astype(BF16),
        b_gate=b_gate,
        cv_w=cv_w[l],
        cv_b=cv_b[l].reshape(1, D_CONV),
        cv_ln_g=cv_ln_g[l].reshape(1, D_CONV),
        cv_ln_b=cv_ln_b[l].reshape(1, D_CONV),
        sgu_ln_g=sgu_ln_g[l].reshape(1, D_SGU),
        sgu_ln_b=sgu_ln_b[l].reshape(1, D_SGU),
        sgu_w=sgu_w[l].astype(BF16),
        sgu_bias=jnp.repeat(sgu_b[l].T, D_SGU // SGU_HEADS, axis=1),
    )


def kernel(x, c, ctx, c_ctx, w_mod, b_mod, norm_g, w_in, w_out, lru_conv_w, lru_conv_b, lru_lam,
           lru_w_r, lru_b_r, lru_w_i, lru_b_i, cv_w, cv_b, cv_ln_g, cv_ln_b, sgu_ln_g, sgu_ln_b,
           sgu_w, sgu_b, final_g):
    assert x.shape[0] == 1 and c.shape[0] == 1 and ctx.shape[0] == 1
    depth = w_mod.shape[0]
    xs = x[0]
    cs = ctx[0]
    cvec = jnp.concatenate(
        [c, c_ctx.reshape(1, D_MODEL), jnp.zeros((SUBLANES - 2, D_MODEL), F32)], axis=0)
    mod = _mod_call(cvec, w_mod, b_mod)
    fg = final_g.reshape(1, D_MODEL)
    zero_state = jnp.zeros((1, D_LRU), F32)
    w_in_bf = w_in.astype(BF16)
    w_out_bf = w_out.astype(BF16)
    for l in range(depth):
        last = l == depth - 1
        p = _layer_params(l, norm_g, lru_conv_w, lru_conv_b, lru_lam, lru_w_r, lru_b_r, lru_w_i,
                          lru_b_i, cv_w, cv_b, cv_ln_g, cv_ln_b, sgu_ln_g, sgu_ln_b, sgu_w, sgu_b)
        mod_x = mod[l, 0:1]
        mod_c = mod[l, 1:2]
        mid_c, yg_c, hf_end = _pass_a_call(cs, mod_c[:, 0:2 * D_MODEL], zero_state, l, w_in_bf, p)
        cs_new, hb_end = _pass_b_call(mid_c, yg_c, cs, mod_c[:, 2 * D_MODEL:], zero_state, l, w_out_bf,
                                      fg, False)
        mid_x, yg_x, _ = _pass_a_call(xs, mod_x[:, 0:2 * D_MODEL], hf_end, l, w_in_bf, p)
        xs, _ = _pass_b_call(mid_x, yg_x, xs, mod_x[:, 2 * D_MODEL:], hb_end, l, w_out_bf, fg, last)
        cs = cs_new
    return xs[None]
```

```python
import functools

import jax
import jax.numpy as jnp
from jax import lax
from jax.experimental import pallas as pl
from jax.experimental.pallas import tpu as pltpu

F32 = jnp.float32
BF16 = jnp.bfloat16

D_MODEL = 2048
EPS = 1e-6
D_LRU = 1024
LRU_HEADS = 16
LRU_HEAD_DIM = D_LRU // LRU_HEADS
LRU_C = 8.0
LRU_CONV = 4
D_CONV = 512
CONV_WIDTH = 31
D_SGU = 512
SGU_HEADS = 8
CHUNK = 128
D_IN = 5120
D_EXT = D_LRU + 2 * D_CONV
O_LRU_G = D_LRU
O_CV = 2 * D_LRU
O_CV_G = O_CV + 2 * D_CONV
O_SGU = O_CV_G + D_CONV
LOG2_E = 1.4426950408889634
M_HF, M_AB, M_BB, M_GL, D_MID = 0, D_LRU, 2 * D_LRU, 3 * D_LRU, 4 * D_LRU

LANES = 128
SUBLANES = 8
HALO = 16
LRU_HALO = 8
TILE = 256
TILE_B = 512
GATE_BLK = 256
VMEM_LIMIT = 56 * 1024 * 1024


def _sigmoid(v):
    return 1.0 / (1.0 + jnp.exp2(v * (-LOG2_E)))


def _silu(v):
    hv = 0.5 * v
    return hv + hv * jnp.tanh(hv)


def _odd_pitch(rows):
    return rows if (rows // SUBLANES) % 2 == 1 else rows + SUBLANES


def _replicate_subchunks(src_ref, col0, n_slab, dst_s, row0, rows, sub_len, pitch):
    for q in range(n_slab):
        cols = slice(col0 + q * LANES, col0 + (q + 1) * LANES)
        for s in range(SUBLANES):
            r = row0 + s * sub_len
            dst_s[q, s * pitch:s * pitch + rows, :] = src_ref[r:r + rows, cols]


def _interleaved_conv(src_s, n_slab, pitch, first, taps, weight, bias, out_s, sub_len):
    for q in range(n_slab):
        cols = slice(q * LANES, (q + 1) * LANES)
        acc = [bias(q)] * sub_len
        for m in range(sub_len + taps - 1):
            v = src_s[q, pl.ds(first + m, SUBLANES, stride=pitch), :]
            for j in range(sub_len):
                k = m - j
                if 0 <= k < taps:
                    acc[j] = acc[j] + weight(k, q) * v
        for j in range(sub_len):
            out_s[SUBLANES * j:SUBLANES * (j + 1), cols] = acc[j]


def _to_natural(val_fn, n_slab, u_s, sub_len, pitch, emit):
    for q in range(n_slab):
        for j in range(sub_len):
            u_s[q, pl.ds(j, SUBLANES, stride=pitch), :] = val_fn(j, q)
        for s in range(SUBLANES):
            emit(s, q, u_s[q, s * pitch:s * pitch + sub_len, :])


def _scan_loop(a_load, b_load, h_store, p_store, sub_len, reverse):
    n_slab = D_LRU // LANES

    hs = [jnp.zeros((SUBLANES, LANES), F32)] * n_slab
    ps = [jnp.ones((SUBLANES, LANES), F32)] * n_slab
    for jj in range(sub_len):
        r = SUBLANES * ((sub_len - 1 - jj) if reverse else jj)
        for q in range(n_slab):
            cols = slice(q * LANES, (q + 1) * LANES)
            a = a_load(r, cols)
            hs[q] = a * hs[q] + b_load(r, cols)
            ps[q] = a * ps[q]
            h_store(r, cols, hs[q])
            p_store(r, cols, ps[q])
    return jnp.concatenate(hs, axis=1), jnp.concatenate(ps, axis=1)


def _chain_states(h_end, p_end, carry_ref, reverse):
    c = carry_ref[...]
    rows = [None] * SUBLANES
    order = range(SUBLANES - 1, -1, -1) if reverse else range(SUBLANES)
    for s in order:
        rows[s] = c
        c = h_end[s:s + 1, :] + p_end[s:s + 1, :] * c
    carry_ref[...] = c
    return jnp.concatenate(rows, axis=0)


def _row_loader(ref, col0=0, row0=0):
    return lambda r, cols: ref[pl.ds(row0 + r, SUBLANES), slice(col0 + cols.start, col0 + cols.stop)]


def _row_storer(ref):
    def store(r, cols, v):
        ref[pl.ds(r, SUBLANES), cols] = v
    return store


def _mod_kernel(c_ref, w_ref, b_ref, o_ref):
    act = _silu(c_ref[...]).astype(BF16)
    o_ref[0] = jnp.dot(act, w_ref[0].astype(BF16), preferred_element_type=F32) + b_ref[0]


def _pass_a_kernel(x_ref, xn_ref, mod_ref, ng_ref, w_ref, lcw_ref, lcb_ref, lam_ref,
                   wg_ref, bg_ref, cvw_ref, cvb_ref, cvg_ref, cvbt_ref, sgg_ref, sgbt_ref,
                   sw_ref, sbias_ref, h0_ref,
                   mid_ref, yg_ref, hend_ref,
                   hn_s, pext_s, cl_s, xl_s, a_s, b_s, cz_s, w8_s, yc_s, u_s, carry_s,
                   *, tile, sub_len, pitch_l, pitch_z, pitch_u):
    i = pl.program_id(0)
    n = pl.num_programs(0)
    ext = tile + 2 * HALO
    w = w_ref.at[0]

    @pl.when(i == 0)
    def _():
        carry_s[...] = h0_ref[...]
        pext_s[tile:tile + HALO, :] = jnp.zeros((HALO, D_EXT), F32)
        for k in range(CONV_WIDTH):
            w8_s[k] = jnp.broadcast_to(cvw_ref[k:k + 1, :], (SUBLANES, D_CONV))

    pext_s[0:HALO, :] = pext_s[tile:tile + HALO, :]

    shift = mod_ref[:, 0:D_MODEL]
    gain = ng_ref[...] * (1.0 + mod_ref[:, D_MODEL:2 * D_MODEL])

    def norm_mod(v):
        ms = jnp.mean(v * v, axis=-1, keepdims=True)
        return v * lax.rsqrt(ms + EPS) * gain + shift

    keep_next = jnp.where(i == n - 1, 0.0, 1.0).astype(F32)
    hn_s[0:tile, :] = norm_mod(x_ref[...]).astype(BF16)
    hn_s[tile:tile + HALO, :] = (norm_mod(xn_ref[...]) * keep_next).astype(BF16)

    pext_s[HALO:ext, 0:D_LRU] = jnp.dot(hn_s[...], w[:, 0:D_LRU], preferred_element_type=F32)
    pext_s[HALO:ext, D_LRU:D_EXT] = jnp.dot(hn_s[...], w[:, O_CV:O_CV_G],
                                            preferred_element_type=F32)

    n_slab = D_LRU // LANES
    _replicate_subchunks(pext_s, 0, n_slab, cl_s, HALO - LRU_HALO, sub_len + 2 * LRU_HALO,
                         sub_len, pitch_l)
    lru_taps = {}

    def lru_weight(k, q):
        if (k, q) not in lru_taps:
            lru_taps[k, q] = jnp.broadcast_to(
                lcw_ref[k:k + 1, q * LANES:(q + 1) * LANES], (SUBLANES, LANES))
        return lru_taps[k, q]

    _interleaved_conv(
        cl_s, n_slab, pitch_l, LRU_HALO - LRU_CONV // 2, LRU_CONV, lru_weight,
        lambda q: jnp.broadcast_to(lcb_ref[:, q * LANES:(q + 1) * LANES], (SUBLANES, LANES)),
        xl_s, sub_len)

    hn_main = hn_s.at[0:tile, :]
    mid_ref[:, M_GL:M_GL + D_LRU] = _silu(
        jnp.dot(hn_main[...], w[:, O_LRU_G:O_CV], preferred_element_type=F32))

    neg_lam = -lam_ref[...]
    softplus = jnp.maximum(neg_lam, 0.0) + jnp.log1p(jnp.exp(-jnp.abs(neg_lam)))
    decay = LRU_C * softplus
    decay_log2 = decay * (-LOG2_E)
    for j in range(D_LRU // GATE_BLK):
        cs = slice(GATE_BLK * j, GATE_BLK * (j + 1))
        xl = xl_s[:, cs]
        g = jnp.dot(xl.astype(BF16), wg_ref[j], preferred_element_type=F32)
        for d in range(2):
            g0 = 2 * GATE_BLK * d
            r = _sigmoid(g[:, g0:g0 + GATE_BLK] + bg_ref[2 * d:2 * d + 1, cs])
            gi = _sigmoid(g[:, g0 + GATE_BLK:g0 + 2 * GATE_BLK] + bg_ref[2 * d + 1:2 * d + 2, cs])
            neg_log_a = decay[d:d + 1, cs] * r
            a = jnp.exp2(decay_log2[d:d + 1, cs] * r)
            sq = jnp.tanh(neg_log_a) * (a * a + 1.0)
            root = jnp.where(sq > 0.0, sq * lax.rsqrt(sq), 0.0)
            bv = root * (gi * xl)
            if d == 0:
                a_s[:, cs] = a
                b_s[:, cs] = bv
            else:
                mid_ref[:, M_AB + GATE_BLK * j:M_AB + GATE_BLK * (j + 1)] = a
                mid_ref[:, M_BB + GATE_BLK * j:M_BB + GATE_BLK * (j + 1)] = bv

    h_end, p_end = _scan_loop(_row_loader(a_s), _row_loader(b_s), _row_storer(b_s), _row_storer(a_s),
                              sub_len, reverse=False)
    c_in = _chain_states(h_end, p_end, carry_s, reverse=False)
    hend_ref[...] = carry_s[...]
    for j in range(sub_len):
        rows = slice(SUBLANES * j, SUBLANES * (j + 1))
        mid_ref[rows, M_HF:M_HF + D_LRU] = b_s[rows, :] + a_s[rows, :] * c_in

    n_slab = D_CONV // LANES
    for q in range(n_slab):
        ca = slice(D_LRU + q * LANES, D_LRU + (q + 1) * LANES)
        cg = slice(D_LRU + D_CONV + q * LANES, D_LRU + D_CONV + (q + 1) * LANES)
        for s in range(SUBLANES):
            r = s * sub_len
            rows = sub_len + 2 * HALO
            cz_s[q, s * pitch_z:s * pitch_z + rows, :] = (
                pext_s[r:r + rows, ca] * _sigmoid(pext_s[r:r + rows, cg]))
    _interleaved_conv(
        cz_s, n_slab, pitch_z, HALO - CONV_WIDTH // 2, CONV_WIDTH,
        lambda k, q: w8_s[k, :, q * LANES:(q + 1) * LANES],
        lambda q: jnp.broadcast_to(cvb_ref[:, q * LANES:(q + 1) * LANES], (SUBLANES, LANES)),
        yc_s, sub_len)
    acc = yc_s[...]
    mu = jnp.mean(acc, axis=-1, keepdims=True)
    cen = acc - mu
    var = jnp.mean(cen * cen, axis=-1, keepdims=True)
    yc_s[...] = _silu(cen * lax.rsqrt(var + EPS) * cvg_ref[...] + cvbt_ref[...])
    g_cv = _silu(jnp.dot(hn_main[...], w[:, O_CV_G:O_SGU], preferred_element_type=F32))

    def emit_cv(s, q, block):
        rows = slice(s * sub_len, (s + 1) * sub_len)
        cols = slice(q * LANES, (q + 1) * LANES)
        yg_ref[rows, cols] = (block * g_cv[rows, cols]).astype(BF16)

    _to_natural(lambda j, q: yc_s[SUBLANES * j:SUBLANES * (j + 1), q * LANES:(q + 1) * LANES],
                n_slab, u_s, sub_len, pitch_u, emit_cv)

    p_sg = jnp.dot(hn_main[...], w[:, O_SGU:D_IN], preferred_element_type=F32)
    zz = jax.nn.gelu(p_sg[:, 0:2 * D_SGU])
    u = zz[:, 0:D_SGU]
    v = zz[:, D_SGU:2 * D_SGU]
    mu = jnp.mean(v, axis=-1, keepdims=True)
    cen = v - mu
    var = jnp.mean(cen * cen, axis=-1, keepdims=True)
    vn = (cen * lax.rsqrt(var + EPS) * sgg_ref[...] + sgbt_ref[...]).astype(BF16)
    n_ch = tile // CHUNK
    lane = lax.broadcasted_iota(jnp.int32, (CHUNK, n_ch * LANES), 1) % LANES
    even_head = lane < (D_SGU // SGU_HEADS)
    g_sg = _silu(p_sg[:, 2 * D_SGU:3 * D_SGU])
    for k in range(D_SGU // LANES):
        cols = slice(LANES * k, LANES * (k + 1))
        rhs = jnp.concatenate([vn[c * CHUNK:(c + 1) * CHUNK, cols] for c in range(n_ch)], axis=1)
        s_even = jnp.dot(sw_ref[2 * k], rhs, preferred_element_type=F32)
        s_odd = jnp.dot(sw_ref[2 * k + 1], rhs, preferred_element_type=F32)
        sel = jnp.where(even_head, s_even, s_odd)
        for c in range(n_ch):
            rows = slice(c * CHUNK, (c + 1) * CHUNK)
            s_c = sel[:, c * LANES:(c + 1) * LANES] + sbias_ref[:, cols]
            yg_ref[rows, D_CONV + LANES * k:D_CONV + LANES * (k + 1)] = (
                u[rows, cols] * s_c * g_sg[rows, cols]).astype(BF16)


def _pass_b_kernel(mid_ref, yg_ref, x_ref, wo_ref, gate_ref, h0_ref, fg_ref,
                   xo_ref, hend_ref, hb_s, p_s, u_s, yl_s, carry_s,
                   *, tile, sub_len, pitch_u, final_norm):
    i = pl.program_id(0)
    wo = wo_ref.at[0]

    @pl.when(i == 0)
    def _():
        carry_s[...] = h0_ref[...]

    unit = SUBLANES * sub_len
    for base in range(tile - unit, -1, -unit):
        h_end, p_end = _scan_loop(
            _row_loader(mid_ref, M_AB, base), _row_loader(mid_ref, M_BB, base),
            _row_storer(hb_s), _row_storer(p_s), sub_len, reverse=True)
        c_in = _chain_states(h_end, p_end, carry_s, reverse=True)

        def y_step(j, q, base=base, c_in=c_in):
            rows = slice(SUBLANES * j, SUBLANES * (j + 1))
            cols = slice(q * LANES, (q + 1) * LANES)
            hf = mid_ref[base + SUBLANES * j:base + SUBLANES * (j + 1),
                         M_HF + q * LANES:M_HF + (q + 1) * LANES]
            return hf + hb_s[rows, cols] + p_s[rows, cols] * c_in[:, cols]

        def emit(s, q, block, base=base):
            rows = slice(base + s * sub_len, base + (s + 1) * sub_len)
            gl = mid_ref[rows, M_GL + q * LANES:M_GL + (q + 1) * LANES]
            yl_s[rows, q * LANES:(q + 1) * LANES] = (block * gl).astype(BF16)

        _to_natural(y_step, D_LRU // LANES, u_s, sub_len, pitch_u, emit)
    hend_ref[...] = carry_s[...]

    o = jnp.dot(yl_s[...], wo[0:D_LRU, :], preferred_element_type=F32)
    o = o + jnp.dot(yg_ref[...], wo[D_LRU:D_MODEL, :], preferred_element_type=F32)
    xn = x_ref[...] + gate_ref[...] * o
    if final_norm:
        ms = jnp.mean(xn * xn, axis=-1, keepdims=True)
        xn = xn * lax.rsqrt(ms + EPS) * fg_ref[...]
    xo_ref[...] = xn


def _const_spec(shape):
    nd = len(shape)
    return pl.BlockSpec(shape, lambda i: (0,) * nd, pipeline_mode=pl.Buffered(1))


def _layer_spec(stacked, layer):
    shape = (1,) + stacked.shape[1:]
    nd = len(shape)
    return pl.BlockSpec(shape, lambda i: (layer,) + (0,) * (nd - 1), pipeline_mode=pl.Buffered(1))


def _mod_call(cvec, w_mod, b_mod):
    depth = w_mod.shape[0]
    bn = 768
    return pl.pallas_call(
        _mod_kernel,
        grid=(depth, 3 * D_MODEL // bn),
        in_specs=[
            pl.BlockSpec((SUBLANES, D_MODEL), lambda l, j: (0, 0)),
            pl.BlockSpec((1, D_MODEL, bn), lambda l, j: (l, 0, j)),
            pl.BlockSpec((1, 1, bn), lambda l, j: (l, 0, j)),
        ],
        out_specs=pl.BlockSpec((1, SUBLANES, bn), lambda l, j: (l, 0, j)),
        out_shape=jax.ShapeDtypeStruct((depth, SUBLANES, 3 * D_MODEL), F32),
        compiler_params=pltpu.CompilerParams(dimension_semantics=("arbitrary", "arbitrary")),
        name="mod_proj",
    )(cvec, w_mod, b_mod.reshape(depth, 1, 3 * D_MODEL))


def _tiling(rows):
    tile = min(TILE, rows)
    assert rows % tile == 0 and tile % (SUBLANES * SUBLANES) == 0 and tile % CHUNK == 0
    sub_len = tile // SUBLANES
    return tile, rows // tile, sub_len


def _pass_a_call(x, mod_row, h0, layer, w_in, p):
    rows = x.shape[0]
    tile, n, sub_len = _tiling(rows)
    pitch_l = _odd_pitch(sub_len + 2 * LRU_HALO)
    pitch_z = _odd_pitch(sub_len + 2 * HALO)
    pitch_u = _odd_pitch(sub_len)
    ext = tile + 2 * HALO
    hb = tile // HALO
    n_hb = rows // HALO
    row_spec = lambda w: pl.BlockSpec((tile, w), lambda i: (i, 0))
    consts = [p["lru_conv_w"], p["lru_conv_b"], p["lru_lam"],
              p["w_gate"], p["b_gate"], p["cv_w"], p["cv_b"], p["cv_ln_g"], p["cv_ln_b"],
              p["sgu_ln_g"], p["sgu_ln_b"], p["sgu_w"], p["sgu_bias"], h0]
    kern = functools.partial(_pass_a_kernel, tile=tile, sub_len=sub_len, pitch_l=pitch_l,
                             pitch_z=pitch_z, pitch_u=pitch_u)
    return pl.pallas_call(
        kern,
        grid=(n,),
        in_specs=[
            row_spec(D_MODEL),
            pl.BlockSpec((HALO, D_MODEL), lambda i: (jnp.minimum((i + 1) * hb, n_hb - 1), 0)),
            _const_spec(mod_row.shape), _const_spec(p["norm_g"].shape), _layer_spec(w_in, layer),
        ] + [_const_spec(c.shape) for c in consts],
        out_specs=[row_spec(D_MID), row_spec(D_LRU), pl.BlockSpec((1, D_LRU), lambda i: (0, 0))],
        out_shape=[
            jax.ShapeDtypeStruct((rows, D_MID), F32),
            jax.ShapeDtypeStruct((rows, D_LRU), BF16),
            jax.ShapeDtypeStruct((1, D_LRU), F32),
        ],
        scratch_shapes=[
            pltpu.VMEM((tile + HALO, D_MODEL), BF16),
            pltpu.VMEM((ext, D_EXT), F32),
            pltpu.VMEM((D_LRU // LANES, SUBLANES * pitch_l, LANES), F32),
            pltpu.VMEM((tile, D_LRU), F32),
            pltpu.VMEM((tile, D_LRU), F32),
            pltpu.VMEM((tile, D_LRU), F32),
            pltpu.VMEM((D_CONV // LANES, SUBLANES * pitch_z, LANES), F32),
            pltpu.VMEM((CONV_WIDTH, SUBLANES, D_CONV), F32),
            pltpu.VMEM((tile, D_CONV), F32),
            pltpu.VMEM((D_CONV // LANES, SUBLANES * pitch_u, LANES), F32),
            pltpu.VMEM((1, D_LRU), F32),
        ],
        compiler_params=pltpu.CompilerParams(
            dimension_semantics=("arbitrary",), vmem_limit_bytes=VMEM_LIMIT),
        name="pass_a",
    )(x, x, mod_row, p["norm_g"], w_in, *consts)


def _pass_b_call(mid, yg, x, gate_row, h0, layer, w_out, final_g, final_norm):
    rows = x.shape[0]
    unit, _, sub_len = _tiling(rows)
    tile = min(TILE_B, rows)
    assert rows % tile == 0 and tile % unit == 0
    n = rows // tile
    pitch_u = _odd_pitch(sub_len)
    row_spec = lambda w: pl.BlockSpec((tile, w), lambda i: (n - 1 - i, 0))
    consts = [gate_row, h0, final_g]
    kern = functools.partial(_pass_b_kernel, tile=tile, sub_len=sub_len, pitch_u=pitch_u,
                             final_norm=final_norm)
    return pl.pallas_call(
        kern,
        grid=(n,),
        in_specs=[row_spec(D_MID), row_spec(D_LRU), row_spec(D_MODEL), _layer_spec(w_out, layer)]
        + [_const_spec(c.shape) for c in consts],
        out_specs=[row_spec(D_MODEL), pl.BlockSpec((1, D_LRU), lambda i: (0, 0))],
        out_shape=[jax.ShapeDtypeStruct((rows, D_MODEL), F32),
                   jax.ShapeDtypeStruct((1, D_LRU), F32)],
        scratch_shapes=[
            pltpu.VMEM((unit, D_LRU), F32),
            pltpu.VMEM((unit, D_LRU), F32),
            pltpu.VMEM((D_LRU // LANES, SUBLANES * pitch_u, LANES), F32),
            pltpu.VMEM((tile, D_LRU), BF16),
            pltpu.VMEM((1, D_LRU), F32),
        ],
        compiler_params=pltpu.CompilerParams(
            dimension_semantics=("arbitrary",), vmem_limit_bytes=VMEM_LIMIT),
        name="pass_b",
    )(mid, yg, x, w_out, *consts)


def _layer_params(l, norm_g, lru_conv_w, lru_conv_b, lru_lam, lru_w_r, lru_b_r, lru_w_i, lru_b_i,
                  cv_w, cv_b, cv_ln_g, cv_ln_b, sgu_ln_g, sgu_ln_b, sgu_w, sgu_b):
    gates = jnp.stack([lru_w_r[l, 0], lru_w_i[l, 0], lru_w_r[l, 1], lru_w_i[l, 1]])
    hpb = GATE_BLK // LRU_HEAD_DIM
    nb = D_LRU // GATE_BLK
    gates = gates.reshape(4, nb, hpb, LRU_HEAD_DIM, LRU_HEAD_DIM)
    bd = jnp.einsum("tjqio,qr->tjqiro", gates, jnp.eye(hpb, dtype=gates.dtype))
    bd = bd.reshape(4, nb, GATE_BLK, GATE_BLK).transpose(1, 2, 0, 3).reshape(nb, GATE_BLK, 4 * GATE_BLK)
    b_gate = jnp.stack([lru_b_r[l, 0], lru_b_i[l, 0], lru_b_r[l, 1], lru_b_i[l, 1]]).reshape(4, D_LRU)
    return dict(
        norm_g=norm_g[l].reshape(1, D_MODEL),
        lru_conv_w=lru_conv_w[l],
        lru_conv_b=lru_conv_b[l].reshape(1, D_LRU),
        lru_lam=lru_lam[l],
        w_gate=bd.astype(BF16),
        b_gate=b_gate,
        cv_w=cv_w[l],
        cv_b=cv_b[l].reshape(1, D_CONV),
        cv_ln_g=cv_ln_g[l].reshape(1, D_CONV),
        cv_ln_b=cv_ln_b[l].reshape(1, D_CONV),
        sgu_ln_g=sgu_ln_g[l].reshape(1, D_SGU),
        sgu_ln_b=sgu_ln_b[l].reshape(1, D_SGU),
        sgu_w=sgu_w[l].astype(BF16),
        sgu_bias=jnp.repeat(sgu_b[l].T, D_SGU // SGU_HEADS, axis=1),
    )


def kernel(x, c, ctx, c_ctx, w_mod, b_mod, norm_g, w_in, w_out, lru_conv_w, lru_conv_b, lru_lam,
           lru_w_r, lru_b_r, lru_w_i, lru_b_i, cv_w, cv_b, cv_ln_g, cv_ln_b, sgu_ln_g, sgu_ln_b,
           sgu_w, sgu_b, final_g):
    assert x.shape[0] == 1 and c.shape[0] == 1 and ctx.shape[0] == 1
    depth = w_mod.shape[0]
    xs = x[0]
    cs = ctx[0]
    cvec = jnp.concatenate(
        [c, c_ctx.reshape(1, D_MODEL), jnp.zeros((SUBLANES - 2, D_MODEL), F32)], axis=0)
    mod = _mod_call(cvec, w_mod, b_mod)
    fg = final_g.reshape(1, D_MODEL)
    zero_state = jnp.zeros((1, D_LRU), F32)
    w_in_bf = w_in.astype(BF16)
    w_out_bf = w_out.astype(BF16)
    for l in range(depth):
        last = l == depth - 1
        p = _layer_params(l, norm_g, lru_conv_w, lru_conv_b, lru_lam, lru_w_r, lru_b_r, lru_w_i,
                          lru_b_i, cv_w, cv_b, cv_ln_g, cv_ln_b, sgu_ln_g, sgu_ln_b, sgu_w, sgu_b)
        mod_x = mod[l, 0:1]
        mod_c = mod[l, 1:2]
        mid_c, yg_c, hf_end = _pass_a_call(cs, mod_c[:, 0:2 * D_MODEL], zero_state, l, w_in_bf, p)
        cs_new, hb_end = _pass_b_call(mid_c, yg_c, cs, mod_c[:, 2 * D_MODEL:], zero_state, l, w_out_bf,
                                      fg, False)
        mid_x, yg_x, _ = _pass_a_call(xs, mod_x[:, 0:2 * D_MODEL], hf_end, l, w_in_bf, p)
        xs, _ = _pass_b_call(mid_x, yg_x, xs, mod_x[:, 2 * D_MODEL:], hb_end, l, w_out_bf, fg, last)
        cs = cs_new
    return xs[None]
```

```python
import functools

import jax
import jax.numpy as jnp
from jax import lax
from jax.experimental import pallas as pl
from jax.experimental.pallas import tpu as pltpu

F32 = jnp.float32
BF16 = jnp.bfloat16

D_MODEL = 2048
EPS = 1e-6
D_LRU = 1024
LRU_HEADS = 16
LRU_HEAD_DIM = D_LRU // LRU_HEADS
LRU_C = 8.0
LRU_CONV = 4
D_CONV = 512
CONV_WIDTH = 31
D_SGU = 512
SGU_HEADS = 8
CHUNK = 128
D_IN = 5120
D_EXT = D_LRU + 2 * D_CONV
O_LRU_G = D_LRU
O_CV = 2 * D_LRU
O_CV_G = O_CV + 2 * D_CONV
O_SGU = O_CV_G + D_CONV
LOG2_E = 1.4426950408889634
M_HF, M_AB, M_BB, M_GL, D_MID = 0, D_LRU, 2 * D_LRU, 3 * D_LRU, 4 * D_LRU

LANES = 128
SUBLANES = 8
HALO = 16
LRU_HALO = 8
TILE = 256
TILE_B = 512
GATE_BLK = 256
VMEM_LIMIT = 56 * 1024 * 1024


def _sigmoid(v):
    return 1.0 / (1.0 + jnp.exp2(v * (-LOG2_E)))


def _silu(v):
    hv = 0.5 * v
    return hv + hv * jnp.tanh(hv)


def _odd_pitch(rows):
    return rows if (rows // SUBLANES) % 2 == 1 else rows + SUBLANES


def _replicate_subchunks(src_ref, col0, n_slab, dst_s, row0, rows, sub_len, pitch):
    for q in range(n_slab):
        cols = slice(col0 + q * LANES, col0 + (q + 1) * LANES)
        for s in range(SUBLANES):
            r = row0 + s * sub_len
            dst_s[q, s * pitch:s * pitch + rows, :] = src_ref[r:r + rows, cols]


def _interleaved_conv(src_s, n_slab, pitch, first, taps, weight, bias, out_s, sub_len):
    for q in range(n_slab):
        cols = slice(q * LANES, (q + 1) * LANES)
        acc = [bias(q)] * sub_len
        for m in range(sub_len + taps - 1):
            v = src_s[q, pl.ds(first + m, SUBLANES, stride=pitch), :]
            for j in range(sub_len):
                k = m - j
                if 0 <= k < taps:
                    acc[j] = acc[j] + weight(k, q) * v
        for j in range(sub_len):
            out_s[SUBLANES * j:SUBLANES * (j + 1), cols] = acc[j]


def _to_natural(val_fn, n_slab, u_s, sub_len, pitch, emit):
    for q in range(n_slab):
        for j in range(sub_len):
            u_s[q, pl.ds(j, SUBLANES, stride=pitch), :] = val_fn(j, q)
        for s in range(SUBLANES):
            emit(s, q, u_s[q, s * pitch:s * pitch + sub_len, :])


def _scan_loop(a_load, b_load, h_store, p_store, sub_len, reverse):
    n_slab = D_LRU // LANES

    hs = [jnp.zeros((SUBLANES, LANES), F32)] * n_slab
    ps = [jnp.ones((SUBLANES, LANES), F32)] * n_slab
    for jj in range(sub_len):
        r = SUBLANES * ((sub_len - 1 - jj) if reverse else jj)
        for q in range(n_slab):
            cols = slice(q * LANES, (q + 1) * LANES)
            a = a_load(r, cols)
            hs[q] = a * hs[q] + b_load(r, cols)
            ps[q] = a * ps[q]
            h_store(r, cols, hs[q])
            p_store(r, cols, ps[q])
    return jnp.concatenate(hs, axis=1), jnp.concatenate(ps, axis=1)


def _chain_states(h_end, p_end, carry_ref, reverse):
    c = carry_ref[...]
    rows = [None] * SUBLANES
    order = range(SUBLANES - 1, -1, -1) if reverse else range(SUBLANES)
    for s in order:
        rows[s] = c
        c = h_end[s:s + 1, :] + p_end[s:s + 1, :] * c
    carry_ref[...] = c
    return jnp.concatenate(rows, axis=0)


def _row_loader(ref, col0=0, row0=0):
    return lambda r, cols: ref[pl.ds(row0 + r, SUBLANES), slice(col0 + cols.start, col0 + cols.stop)]


def _row_storer(ref):
    def store(r, cols, v):
        ref[pl.ds(r, SUBLANES), cols] = v
    return store


def _mod_kernel(c_ref, w_ref, b_ref, o_ref):
    act = _silu(c_ref[...]).astype(BF16)
    o_ref[0] = jnp.dot(act, w_ref[0].astype(BF16), preferred_element_type=F32) + b_ref[0]


def _pass_a_kernel(x_ref, xn_ref, mod_ref, ng_ref, w_ref, lcw_ref, lcb_ref, lam_ref,
                   wg_ref, bg_ref, cvw_ref, cvb_ref, cvg_ref, cvbt_ref, sgg_ref, sgbt_ref,
                   sw_ref, sbias_ref, h0_ref,
                   mid_ref, yg_ref, hend_ref,
                   hn_s, pext_s, cl_s, xl_s, a_s, b_s, cz_s, w8_s, yc_s, u_s, carry_s,
                   *, tile, sub_len, pitch_l, pitch_z, pitch_u):
    i = pl.program_id(0)
    n = pl.num_programs(0)
    ext = tile + 2 * HALO
    w = w_ref.at[0]

    @pl.when(i == 0)
    def _():
        carry_s[...] = h0_ref[...]
        pext_s[tile:tile + HALO, :] = jnp.zeros((HALO, D_EXT), F32)
        for k in range(CONV_WIDTH):
            w8_s[k] = jnp.broadcast_to(cvw_ref[k:k + 1, :], (SUBLANES, D_CONV))

    pext_s[0:HALO, :] = pext_s[tile:tile + HALO, :]

    shift = mod_ref[:, 0:D_MODEL]
    gain = ng_ref[...] * (1.0 + mod_ref[:, D_MODEL:2 * D_MODEL])

    def norm_mod(v):
        ms = jnp.mean(v * v, axis=-1, keepdims=True)
        return v * lax.rsqrt(ms + EPS) * gain + shift

    keep_next = jnp.where(i == n - 1, 0.0, 1.0).astype(F32)
    hn_s[0:tile, :] = norm_mod(x_ref[...]).astype(BF16)
    hn_s[tile:tile + HALO, :] = (norm_mod(xn_ref[...]) * keep_next).astype(BF16)

    pext_s[HALO:ext, 0:D_LRU] = jnp.dot(hn_s[...], w[:, 0:D_LRU], preferred_element_type=F32)
    pext_s[HALO:ext, D_LRU:D_EXT] = jnp.dot(hn_s[...], w[:, O_CV:O_CV_G],
                                            preferred_element_type=F32)

    n_slab = D_LRU // LANES
    _replicate_subchunks(pext_s, 0, n_slab, cl_s, HALO - LRU_HALO, sub_len + 2 * LRU_HALO,
                         sub_len, pitch_l)
    lru_taps = {}

    def lru_weight(k, q):
        if (k, q) not in lru_taps:
            lru_taps[k, q] = jnp.broadcast_to(
                lcw_ref[k:k + 1, q * LANES:(q + 1) * LANES], (SUBLANES, LANES))
        return lru_taps[k, q]

    _interleaved_conv(
        cl_s, n_slab, pitch_l, LRU_HALO - LRU_CONV // 2, LRU_CONV, lru_weight,
        lambda q: jnp.broadcast_to(lcb_ref[:, q * LANES:(q + 1) * LANES], (SUBLANES, LANES)),
        xl_s, sub_len)

    hn_main = hn_s.at[0:tile, :]
    mid_ref[:, M_GL:M_GL + D_LRU] = _silu(
        jnp.dot(hn_main[...], w[:, O_LRU_G:O_CV], preferred_element_type=F32))

    neg_lam = -lam_ref[...]
    softplus = jnp.maximum(neg_lam, 0.0) + jnp.log1p(jnp.exp(-jnp.abs(neg_lam)))
    decay = LRU_C * softplus
    decay_log2 = decay * (-LOG2_E)
    for j in range(D_LRU // GATE_BLK):
        cs = slice(GATE_BLK * j, GATE_BLK * (j + 1))
        xl = xl_s[:, cs]
        g = jnp.dot(xl.astype(BF16), wg_ref[j], preferred_element_type=F32)
        for d in range(2):
            g0 = 2 * GATE_BLK * d
            r = _sigmoid(g[:, g0:g0 + GATE_BLK] + bg_ref[2 * d:2 * d + 1, cs])
            gi = _sigmoid(g[:, g0 + GATE_BLK:g0 + 2 * GATE_BLK] + bg_ref[2 * d + 1:2 * d + 2, cs])
            neg_log_a = decay[d:d + 1, cs] * r
            a = jnp.exp2(decay_log2[d:d + 1, cs] * r)
            sq = jnp.tanh(neg_log_a) * (a * a + 1.0)
            root = jnp.where(sq > 0.0, sq * lax.rsqrt(sq), 0.0)
            bv = root * (gi * xl)
            if d == 0:
                a_s[:, cs] = a
                b_s[:, cs] = bv
            else:
                mid_ref[:, M_AB + GATE_BLK * j:M_AB + GATE_BLK * (j + 1)] = a
                mid_ref[:, M_BB + GATE_BLK * j:M_BB + GATE_BLK * (j + 1)] = bv

    h_end, p_end = _scan_loop(_row_loader(a_s), _row_loader(b_s), _row_storer(b_s), _row_storer(a_s),
                              sub_len, reverse=False)
    c_in = _chain_states(h_end, p_end, carry_s, reverse=False)
    hend_ref[...] = carry_s[...]
    for j in range(sub_len):
        rows = slice(SUBLANES * j, SUBLANES * (j + 1))
        mid_ref[rows, M_HF:M_HF + D_LRU] = b_s[rows, :] + a_s[rows, :] * c_in

    n_slab = D_CONV // LANES
    for q in range(n_slab):
        ca = slice(D_LRU + q * LANES, D_LRU + (q + 1) * LANES)
        cg = slice(D_LRU + D_CONV + q * LANES, D_LRU + D_CONV + (q + 1) * LANES)
        for s in range(SUBLANES):
            r = s * sub_len
            rows = sub_len + 2 * HALO
            cz_s[q, s * pitch_z:s * pitch_z + rows, :] = (
                pext_s[r:r + rows, ca] * _sigmoid(pext_s[r:r + rows, cg]))
    _interleaved_conv(
        cz_s, n_slab, pitch_z, HALO - CONV_WIDTH // 2, CONV_WIDTH,
        lambda k, q: w8_s[k, :, q * LANES:(q + 1) * LANES],
        lambda q: jnp.broadcast_to(cvb_ref[:, q * LANES:(q + 1) * LANES], (SUBLANES, LANES)),
        yc_s, sub_len)
    acc = yc_s[...]
    mu = jnp.mean(acc, axis=-1, keepdims=True)
    cen = acc - mu
    var = jnp.mean(cen * cen, axis=-1, keepdims=True)
    yc_s[...] = _silu(cen * lax.rsqrt(var + EPS) * cvg_ref[...] + cvbt_ref[...])
    g_cv = _silu(jnp.dot(hn_main[...], w[:, O_CV_G:O_SGU], preferred_element_type=F32))

    def emit_cv(s, q, block):
        rows = slice(s * sub_len, (s + 1) * sub_len)
        cols = slice(q * LANES, (q + 1) * LANES)
        yg_ref[rows, cols] = (block * g_cv[rows, cols]).astype(BF16)

    _to_natural(lambda j, q: yc_s[SUBLANES * j:SUBLANES * (j + 1), q * LANES:(q + 1) * LANES],
                n_slab, u_s, sub_len, pitch_u, emit_cv)

    p_sg = jnp.dot(hn_main[...], w[:, O_SGU:D_IN], preferred_element_type=F32)
    zz = jax.nn.gelu(p_sg[:, 0:2 * D_SGU])
    u = zz[:, 0:D_SGU]
    v = zz[:, D_SGU:2 * D_SGU]
    mu = jnp.mean(v, axis=-1, keepdims=True)
    cen = v - mu
    var = jnp.mean(cen * cen, axis=-1, keepdims=True)
    vn = (cen * lax.rsqrt(var + EPS) * sgg_ref[...] + sgbt_ref[...]).astype(BF16)
    n_ch = tile // CHUNK
    lane = lax.broadcasted_iota(jnp.int32, (CHUNK, n_ch * LANES), 1) % LANES
    even_head = lane < (D_SGU // SGU_HEADS)
    g_sg = _silu(p_sg[:, 2 * D_SGU:3 * D_SGU])
    for k in range(D_SGU // LANES):
        cols = slice(LANES * k, LANES * (k + 1))
        rhs = jnp.concatenate([vn[c * CHUNK:(c + 1) * CHUNK, cols] for c in range(n_ch)], axis=1)
        s_even = jnp.dot(sw_ref[2 * k], rhs, preferred_element_type=F32)
        s_odd = jnp.dot(sw_ref[2 * k + 1], rhs, preferred_element_type=F32)
        sel = jnp.where(even_head, s_even, s_odd)
        for c in range(n_ch):
            rows = slice(c * CHUNK, (c + 1) * CHUNK)
            s_c = sel[:, c * LANES:(c + 1) * LANES] + sbias_ref[:, cols]
            yg_ref[rows, D_CONV + LANES * k:D_CONV + LANES * (k + 1)] = (
                u[rows, cols] * s_c * g_sg[rows, cols]).astype(BF16)


def _pass_b_kernel(mid_ref, yg_ref, x_ref, wo_ref, gate_ref, h0_ref, fg_ref,
                   xo_ref, hend_ref, hb_s, p_s, u_s, yl_s, carry_s,
                   *, tile, sub_len, pitch_u, final_norm):
    i = pl.program_id(0)
    wo = wo_ref.at[0]

    @pl.when(i == 0)
    def _():
        carry_s[...] = h0_ref[...]

    unit = SUBLANES * sub_len
    for base in range(tile - unit, -1, -unit):
        h_end, p_end = _scan_loop(
            _row_loader(mid_ref, M_AB, base), _row_loader(mid_ref, M_BB, base),
            _row_storer(hb_s), _row_storer(p_s), sub_len, reverse=True)
        c_in = _chain_states(h_end, p_end, carry_s, reverse=True)

        def y_step(j, q, base=base, c_in=c_in):
            rows = slice(SUBLANES * j, SUBLANES * (j + 1))
            cols = slice(q * LANES, (q + 1) * LANES)
            hf = mid_ref[base + SUBLANES * j:base + SUBLANES * (j + 1),
                         M_HF + q * LANES:M_HF + (q + 1) * LANES]
            return hf + hb_s[rows, cols] + p_s[rows, cols] * c_in[:, cols]

        def emit(s, q, block, base=base):
            rows = slice(base + s * sub_len, base + (s + 1) * sub_len)
            gl = mid_ref[rows, M_GL + q * LANES:M_GL + (q + 1) * LANES]
            yl_s[rows, q * LANES:(q + 1) * LANES] = (block * gl).astype(BF16)

        _to_natural(y_step, D_LRU // LANES, u_s, sub_len, pitch_u, emit)
    hend_ref[...] = carry_s[...]

    o = jnp.dot(yl_s[...], wo[0:D_LRU, :], preferred_element_type=F32)
    o = o + jnp.dot(yg_ref[...], wo[D_LRU:D_MODEL, :], preferred_element_type=F32)
    xn = x_ref[...] + gate_ref[...] * o
    if final_norm:
        ms = jnp.mean(xn * xn, axis=-1, keepdims=True)
        xn = xn * lax.rsqrt(ms + EPS) * fg_ref[...]
    xo_ref[...] = xn


def _const_spec(shape):
    nd = len(shape)
    return pl.BlockSpec(shape, lambda i: (0,) * nd)


def _layer_spec(stacked, layer):
    shape = (1,) + stacked.shape[1:]
    nd = len(shape)
    return pl.BlockSpec(shape, lambda i: (layer,) + (0,) * (nd - 1), pipeline_mode=pl.Buffered(1))


def _mod_call(cvec, w_mod, b_mod):
    depth = w_mod.shape[0]
    bn = 768
    return pl.pallas_call(
        _mod_kernel,
        grid=(depth, 3 * D_MODEL // bn),
        in_specs=[
            pl.BlockSpec((SUBLANES, D_MODEL), lambda l, j: (0, 0)),
            pl.BlockSpec((1, D_MODEL, bn), lambda l, j: (l, 0, j)),
            pl.BlockSpec((1, 1, bn), lambda l, j: (l, 0, j)),
        ],
        out_specs=pl.BlockSpec((1, SUBLANES, bn), lambda l, j: (l, 0, j)),
        out_shape=jax.ShapeDtypeStruct((depth, SUBLANES, 3 * D_MODEL), F32),
        compiler_params=pltpu.CompilerParams(dimension_semantics=("arbitrary", "arbitrary")),
        name="mod_proj",
    )(cvec, w_mod, b_mod.reshape(depth, 1, 3 * D_MODEL))


def _tiling(rows):
    tile = min(TILE, rows)
    assert rows % tile == 0 and tile % (SUBLANES * SUBLANES) == 0 and tile % CHUNK == 0
    sub_len = tile // SUBLANES
    return tile, rows // tile, sub_len


def _pass_a_call(x, mod_row, h0, layer, w_in, p):
    rows = x.shape[0]
    tile, n, sub_len = _tiling(rows)
    pitch_l = _odd_pitch(sub_len + 2 * LRU_HALO)
    pitch_z = _odd_pitch(sub_len + 2 * HALO)
    pitch_u = _odd_pitch(sub_len)
    ext = tile + 2 * HALO
    hb = tile // HALO
    n_hb = rows // HALO
    row_spec = lambda w: pl.BlockSpec((tile, w), lambda i: (i, 0))
    consts = [p["lru_conv_w"], p["lru_conv_b"], p["lru_lam"],
              p["w_gate"], p["b_gate"], p["cv_w"], p["cv_b"], p["cv_ln_g"], p["cv_ln_b"],
              p["sgu_ln_g"], p["sgu_ln_b"], p["sgu_w"], p["sgu_bias"], h0]
    kern = functools.partial(_pass_a_kernel, tile=tile, sub_len=sub_len, pitch_l=pitch_l,
                             pitch_z=pitch_z, pitch_u=pitch_u)
    return pl.pallas_call(
        kern,
        grid=(n,),
        in_specs=[
            row_spec(D_MODEL),
            pl.BlockSpec((HALO, D_MODEL), lambda i: (jnp.minimum((i + 1) * hb, n_hb - 1), 0)),
            _const_spec(mod_row.shape), _const_spec(p["norm_g"].shape), _layer_spec(w_in, layer),
        ] + [_const_spec(c.shape) for c in consts],
        out_specs=[row_spec(D_MID), row_spec(D_LRU), pl.BlockSpec((1, D_LRU), lambda i: (0, 0))],
        out_shape=[
            jax.ShapeDtypeStruct((rows, D_MID), F32),
            jax.ShapeDtypeStruct((rows, D_LRU), BF16),
            jax.ShapeDtypeStruct((1, D_LRU), F32),
        ],
        scratch_shapes=[
            pltpu.VMEM((tile + HALO, D_MODEL), BF16),
            pltpu.VMEM((ext, D_EXT), F32),
            pltpu.VMEM((D_LRU // LANES, SUBLANES * pitch_l, LANES), F32),
            pltpu.VMEM((tile, D_LRU), F32),
            pltpu.VMEM((tile, D_LRU), F32),
            pltpu.VMEM((tile, D_LRU), F32),
            pltpu.VMEM((D_CONV // LANES, SUBLANES * pitch_z, LANES), F32),
            pltpu.VMEM((CONV_WIDTH, SUBLANES, D_CONV), F32),
            pltpu.VMEM((tile, D_CONV), F32),
            pltpu.VMEM((D_CONV // LANES, SUBLANES * pitch_u, LANES), F32),
            pltpu.VMEM((1, D_LRU), F32),
        ],
        compiler_params=pltpu.CompilerParams(
            dimension_semantics=("arbitrary",), vmem_limit_bytes=VMEM_LIMIT),
        name="pass_a",
    )(x, x, mod_row, p["norm_g"], w_in, *consts)


def _pass_b_call(mid, yg, x, gate_row, h0, layer, w_out, final_g, final_norm):
    rows = x.shape[0]
    unit, _, sub_len = _tiling(rows)
    tile = min(TILE_B, rows)
    assert rows % tile == 0 and tile % unit == 0
    n = rows // tile
    pitch_u = _odd_pitch(sub_len)
    row_spec = lambda w: pl.BlockSpec((tile, w), lambda i: (n - 1 - i, 0))
    consts = [gate_row, h0, final_g]
    kern = functools.partial(_pass_b_kernel, tile=tile, sub_len=sub_len, pitch_u=pitch_u,
                             final_norm=final_norm)
    return pl.pallas_call(
        kern,
        grid=(n,),
        in_specs=[row_spec(D_MID), row_spec(D_LRU), row_spec(D_MODEL), _layer_spec(w_out, layer)]
        + [_const_spec(c.shape) for c in consts],
        out_specs=[row_spec(D_MODEL), pl.BlockSpec((1, D_LRU), lambda i: (0, 0))],
        out_shape=[jax.ShapeDtypeStruct((rows, D_MODEL), F32),
                   jax.ShapeDtypeStruct((1, D_LRU), F32)],
        scratch_shapes=[
            pltpu.VMEM((unit, D_LRU), F32),
            pltpu.VMEM((unit, D_LRU), F32),
            pltpu.VMEM((D_LRU // LANES, SUBLANES * pitch_u, LANES), F32),
            pltpu.VMEM((tile, D_LRU), BF16),
            pltpu.VMEM((1, D_LRU), F32),
        ],
        compiler_params=pltpu.CompilerParams(
            dimension_semantics=("arbitrary",), vmem_limit_bytes=VMEM_LIMIT),
        name="pass_b",
    )(mid, yg, x, w_out, *consts)


def _layer_params(l, norm_g, lru_conv_w, lru_conv_b, lru_lam, lru_w_r, lru_b_r, lru_w_i, lru_b_i,
                  cv_w, cv_b, cv_ln_g, cv_ln_b, sgu_ln_g, sgu_ln_b, sgu_w, sgu_b):
    gates = jnp.stack([lru_w_r[l, 0], lru_w_i[l, 0], lru_w_r[l, 1], lru_w_i[l, 1]])
    hpb = GATE_BLK // LRU_HEAD_DIM
    nb = D_LRU // GATE_BLK
    gates = gates.reshape(4, nb, hpb, LRU_HEAD_DIM, LRU_HEAD_DIM)
    bd = jnp.einsum("tjqio,qr->tjqiro", gates, jnp.eye(hpb, dtype=gates.dtype))
    bd = bd.reshape(4, nb, GATE_BLK, GATE_BLK).transpose(1, 2, 0, 3).reshape(nb, GATE_BLK, 4 * GATE_BLK)
    b_gate = jnp.stack([lru_b_r[l, 0], lru_b_i[l, 0], lru_b_r[l, 1], lru_b_i[l, 1]]).reshape(4, D_LRU)
    return dict(
        norm_g=norm_g[l].reshape(1, D_MODEL),
        lru_conv_w=lru_conv_w[l],
        lru_conv_b=lru_conv_b[l].reshape(1, D_LRU),
        lru_lam=lru_lam[l],
        w_gate=bd.astype(BF16),
        b_gate=b_gate,
        cv_w=cv_w[l],
        cv_b=cv_b[l].reshape(1, D_CONV),
        cv_ln_g=cv_ln_g[l].reshape(1, D_CONV),
        cv_ln_b=cv_ln_b[l].reshape(1, D_CONV),
        sgu_ln_g=sgu_ln_g[l].reshape(1, D_SGU),
        sgu_ln_b=sgu_ln_b[l].reshape(1, D_SGU),
        sgu_w=sgu_w[l].astype(BF16),
        sgu_bias=jnp.repeat(sgu_b[l].T, D_SGU // SGU_HEADS, axis=1),
    )


def kernel(x, c, ctx, c_ctx, w_mod, b_mod, norm_g, w_in, w_out, lru_conv_w, lru_conv_b, lru_lam,
           lru_w_r, lru_b_r, lru_w_i, lru_b_i, cv_w, cv_b, cv_ln_g, cv_ln_b, sgu_ln_g, sgu_ln_b,
           sgu_w, sgu_b, final_g):
    assert x.shape[0] == 1 and c.shape[0] == 1 and ctx.shape[0] == 1
    depth = w_mod.shape[0]
    xs = x[0]
    cs = ctx[0]
    cvec = jnp.concatenate(
        [c, c_ctx.reshape(1, D_MODEL), jnp.zeros((SUBLANES - 2, D_MODEL), F32)], axis=0)
    mod = _mod_call(cvec, w_mod, b_mod)
    fg = final_g.reshape(1, D_MODEL)
    zero_state = jnp.zeros((1, D_LRU), F32)
    w_in_bf = w_in.astype(BF16)
    w_out_bf = w_out.astype(BF16)
    for l in range(depth):
        last = l == depth - 1
        p = _layer_params(l, norm_g, lru_conv_w, lru_conv_b, lru_lam, lru_w_r, lru_b_r, lru_w_i,
                          lru_b_i, cv_w, cv_b, cv_ln_g, cv_ln_b, sgu_ln_g, sgu_ln_b, sgu_w, sgu_b)
        mod_x = mod[l, 0:1]
        mod_c = mod[l, 1:2]
        mid_c, yg_c, hf_end = _pass_a_call(cs, mod_c[:, 0:2 * D_MODEL], zero_state, l, w_in_bf, p)
        cs_new, hb_end = _pass_b_call(mid_c, yg_c, cs, mod_c[:, 2 * D_MODEL:], zero_state, l, w_out_bf,
                                      fg, False)
        mid_x, yg_x, _ = _pass_a_call(xs, mod_x[:, 0:2 * D_MODEL], hf_end, l, w_in_bf, p)
        xs, _ = _pass_b_call(mid_x, yg_x, xs, mod_x[:, 2 * D_MODEL:], hb_end, l, w_out_bf, fg, last)
        cs = cs_new
    return xs[None]
```

```python
import functools

import jax
import jax.numpy as jnp
from jax import lax
from jax.experimental import pallas as pl
from jax.experimental.pallas import tpu as pltpu

F32 = jnp.float32
BF16 = jnp.bfloat16

D_MODEL = 2048
EPS = 1e-6
D_LRU = 1024
LRU_HEADS = 16
LRU_HEAD_DIM = D_LRU // LRU_HEADS
LRU_C = 8.0
LRU_CONV = 4
D_CONV = 512
CONV_WIDTH = 31
D_SGU = 512
SGU_HEADS = 8
CHUNK = 128
D_IN = 5120
D_EXT = D_LRU + 2 * D_CONV
O_LRU_G = D_LRU
O_CV = 2 * D_LRU
O_CV_G = O_CV + 2 * D_CONV
O_SGU = O_CV_G + D_CONV
LOG2_E = 1.4426950408889634
M_HF, M_AB, M_BB, D_MID = 0, D_LRU, 2 * D_LRU, 3 * D_LRU
Y_MIX, Y_GL, D_YG = 0, D_LRU, 2 * D_LRU

LANES = 128
SUBLANES = 8
HALO = 16
LRU_HALO = 8
TILE = 256
TILE_B = 512
GATE_BLK = 256
VMEM_LIMIT = 56 * 1024 * 1024


def _sigmoid(v):
    return 1.0 / (1.0 + jnp.exp2(v * (-LOG2_E)))


def _silu(v):
    hv = 0.5 * v
    return hv + hv * jnp.tanh(hv)


def _odd_pitch(rows):
    return rows if (rows // SUBLANES) % 2 == 1 else rows + SUBLANES


def _replicate_subchunks(src_ref, col0, n_slab, dst_s, row0, rows, sub_len, pitch):
    for q in range(n_slab):
        cols = slice(col0 + q * LANES, col0 + (q + 1) * LANES)
        for s in range(SUBLANES):
            r = row0 + s * sub_len
            dst_s[q, s * pitch:s * pitch + rows, :] = src_ref[r:r + rows, cols]


def _interleaved_conv(src_s, n_slab, pitch, first, taps, weight, bias, out_s, sub_len):
    for q in range(n_slab):
        cols = slice(q * LANES, (q + 1) * LANES)
        acc = [bias(q)] * sub_len
        for m in range(sub_len + taps - 1):
            v = src_s[q, pl.ds(first + m, SUBLANES, stride=pitch), :]
            for j in range(sub_len):
                k = m - j
                if 0 <= k < taps:
                    acc[j] = acc[j] + weight(k, q) * v
        for j in range(sub_len):
            out_s[SUBLANES * j:SUBLANES * (j + 1), cols] = acc[j]


def _to_natural(val_fn, n_slab, u_s, sub_len, pitch, emit):
    for q in range(n_slab):
        for j in range(sub_len):
            u_s[q, pl.ds(j, SUBLANES, stride=pitch), :] = val_fn(j, q)
        for s in range(SUBLANES):
            emit(s, q, u_s[q, s * pitch:s * pitch + sub_len, :])


def _scan_loop(a_load, b_load, h_store, p_store, sub_len, reverse):
    n_slab = D_LRU // LANES

    hs = [jnp.zeros((SUBLANES, LANES), F32)] * n_slab
    ps = [jnp.ones((SUBLANES, LANES), F32)] * n_slab
    for jj in range(sub_len):
        r = SUBLANES * ((sub_len - 1 - jj) if reverse else jj)
        for q in range(n_slab):
            cols = slice(q * LANES, (q + 1) * LANES)
            a = a_load(r, cols)
            hs[q] = a * hs[q] + b_load(r, cols)
            ps[q] = a * ps[q]
            h_store(r, cols, hs[q])
            p_store(r, cols, ps[q])
    return jnp.concatenate(hs, axis=1), jnp.concatenate(ps, axis=1)


def _chain_states(h_end, p_end, carry_ref, reverse):
    c = carry_ref[...]
    rows = [None] * SUBLANES
    order = range(SUBLANES - 1, -1, -1) if reverse else range(SUBLANES)
    for s in order:
        rows[s] = c
        c = h_end[s:s + 1, :] + p_end[s:s + 1, :] * c
    carry_ref[...] = c
    return jnp.concatenate(rows, axis=0)


def _row_loader(ref, col0=0, row0=0):
    return lambda r, cols: ref[pl.ds(row0 + r, SUBLANES), slice(col0 + cols.start, col0 + cols.stop)]


def _row_storer(ref):
    def store(r, cols, v):
        ref[pl.ds(r, SUBLANES), cols] = v
    return store


def _mod_kernel(c_ref, w_ref, b_ref, o_ref):
    act = _silu(c_ref[...]).astype(BF16)
    o_ref[0] = jnp.dot(act, w_ref[0].astype(BF16), preferred_element_type=F32) + b_ref[0]


def _pass_a_kernel(x_ref, xn_ref, mod_ref, ng_ref, w_ref, lcw_ref, lcb_ref, lam_ref,
                   wg_ref, bg_ref, cvw_ref, cvb_ref, cvg_ref, cvbt_ref, sgg_ref, sgbt_ref,
                   sw_ref, sbias_ref, h0_ref,
                   mid_ref, yg_ref, hend_ref,
                   hn_s, pext_s, cl_s, xl_s, a_s, b_s, cz_s, w8_s, yc_s, u_s, carry_s,
                   *, tile, sub_len, pitch_l, pitch_z, pitch_u):
    i = pl.program_id(0)
    n = pl.num_programs(0)
    ext = tile + 2 * HALO
    w = w_ref.at[0]

    @pl.when(i == 0)
    def _():
        carry_s[...] = h0_ref[...]
        pext_s[tile:tile + HALO, :] = jnp.zeros((HALO, D_EXT), F32)
        for k in range(CONV_WIDTH):
            w8_s[k] = jnp.broadcast_to(cvw_ref[k:k + 1, :], (SUBLANES, D_CONV))

    pext_s[0:HALO, :] = pext_s[tile:tile + HALO, :]

    shift = mod_ref[:, 0:D_MODEL]
    gain = ng_ref[...] * (1.0 + mod_ref[:, D_MODEL:2 * D_MODEL])

    def norm_mod(v):
        ms = jnp.mean(v * v, axis=-1, keepdims=True)
        return v * lax.rsqrt(ms + EPS) * gain + shift

    keep_next = jnp.where(i == n - 1, 0.0, 1.0).astype(F32)
    hn_s[0:tile, :] = norm_mod(x_ref[...]).astype(BF16)
    hn_s[tile:tile + HALO, :] = (norm_mod(xn_ref[...]) * keep_next).astype(BF16)

    pext_s[HALO:ext, 0:D_LRU] = jnp.dot(hn_s[...], w[:, 0:D_LRU], preferred_element_type=F32)
    pext_s[HALO:ext, D_LRU:D_EXT] = jnp.dot(hn_s[...], w[:, O_CV:O_CV_G],
                                            preferred_element_type=F32)

    n_slab = D_LRU // LANES
    _replicate_subchunks(pext_s, 0, n_slab, cl_s, HALO - LRU_HALO, sub_len + 2 * LRU_HALO,
                         sub_len, pitch_l)
    lru_taps = {}

    def lru_weight(k, q):
        if (k, q) not in lru_taps:
            lru_taps[k, q] = jnp.broadcast_to(
                lcw_ref[k:k + 1, q * LANES:(q + 1) * LANES], (SUBLANES, LANES))
        return lru_taps[k, q]

    _interleaved_conv(
        cl_s, n_slab, pitch_l, LRU_HALO - LRU_CONV // 2, LRU_CONV, lru_weight,
        lambda q: jnp.broadcast_to(lcb_ref[:, q * LANES:(q + 1) * LANES], (SUBLANES, LANES)),
        xl_s, sub_len)

    hn_main = hn_s.at[0:tile, :]
    yg_ref[:, Y_GL:Y_GL + D_LRU] = _silu(
        jnp.dot(hn_main[...], w[:, O_LRU_G:O_CV], preferred_element_type=F32)).astype(BF16)

    neg_lam = -lam_ref[...]
    softplus = jnp.maximum(neg_lam, 0.0) + jnp.log1p(jnp.exp(-jnp.abs(neg_lam)))
    decay = LRU_C * softplus
    decay_log2 = decay * (-LOG2_E)
    for j in range(D_LRU // GATE_BLK):
        cs = slice(GATE_BLK * j, GATE_BLK * (j + 1))
        xl = xl_s[:, cs]
        g = jnp.dot(xl.astype(BF16), wg_ref[j], preferred_element_type=F32)
        for d in range(2):
            g0 = 2 * GATE_BLK * d
            r = _sigmoid(g[:, g0:g0 + GATE_BLK] + bg_ref[2 * d:2 * d + 1, cs])
            gi = _sigmoid(g[:, g0 + GATE_BLK:g0 + 2 * GATE_BLK] + bg_ref[2 * d + 1:2 * d + 2, cs])
            neg_log_a = decay[d:d + 1, cs] * r
            a = jnp.exp2(decay_log2[d:d + 1, cs] * r)
            sq = jnp.tanh(neg_log_a) * (a * a + 1.0)
            root = jnp.where(sq > 0.0, sq * lax.rsqrt(sq), 0.0)
            bv = root * (gi * xl)
            if d == 0:
                a_s[:, cs] = a
                b_s[:, cs] = bv
            else:
                mid_ref[:, M_AB + GATE_BLK * j:M_AB + GATE_BLK * (j + 1)] = a
                mid_ref[:, M_BB + GATE_BLK * j:M_BB + GATE_BLK * (j + 1)] = bv

    h_end, p_end = _scan_loop(_row_loader(a_s), _row_loader(b_s), _row_storer(b_s), _row_storer(a_s),
                              sub_len, reverse=False)
    c_in = _chain_states(h_end, p_end, carry_s, reverse=False)
    hend_ref[...] = carry_s[...]
    for j in range(sub_len):
        rows = slice(SUBLANES * j, SUBLANES * (j + 1))
        mid_ref[rows, M_HF:M_HF + D_LRU] = b_s[rows, :] + a_s[rows, :] * c_in

    n_slab = D_CONV // LANES
    for q in range(n_slab):
        ca = slice(D_LRU + q * LANES, D_LRU + (q + 1) * LANES)
        cg = slice(D_LRU + D_CONV + q * LANES, D_LRU + D_CONV + (q + 1) * LANES)
        for s in range(SUBLANES):
            r = s * sub_len
            rows = sub_len + 2 * HALO
            cz_s[q, s * pitch_z:s * pitch_z + rows, :] = (
                pext_s[r:r + rows, ca] * _sigmoid(pext_s[r:r + rows, cg]))
    _interleaved_conv(
        cz_s, n_slab, pitch_z, HALO - CONV_WIDTH // 2, CONV_WIDTH,
        lambda k, q: w8_s[k, :, q * LANES:(q + 1) * LANES],
        lambda q: jnp.broadcast_to(cvb_ref[:, q * LANES:(q + 1) * LANES], (SUBLANES, LANES)),
        yc_s, sub_len)
    acc = yc_s[...]
    mu = jnp.mean(acc, axis=-1, keepdims=True)
    cen = acc - mu
    var = jnp.mean(cen * cen, axis=-1, keepdims=True)
    yc_s[...] = _silu(cen * lax.rsqrt(var + EPS) * cvg_ref[...] + cvbt_ref[...])
    g_cv = _silu(jnp.dot(hn_main[...], w[:, O_CV_G:O_SGU], preferred_element_type=F32))

    def emit_cv(s, q, block):
        rows = slice(s * sub_len, (s + 1) * sub_len)
        cols = slice(q * LANES, (q + 1) * LANES)
        yg_ref[rows, cols] = (block * g_cv[rows, cols]).astype(BF16)

    _to_natural(lambda j, q: yc_s[SUBLANES * j:SUBLANES * (j + 1), q * LANES:(q + 1) * LANES],
                n_slab, u_s, sub_len, pitch_u, emit_cv)

    p_sg = jnp.dot(hn_main[...], w[:, O_SGU:D_IN], preferred_element_type=F32)
    zz = jax.nn.gelu(p_sg[:, 0:2 * D_SGU])
    u = zz[:, 0:D_SGU]
    v = zz[:, D_SGU:2 * D_SGU]
    mu = jnp.mean(v, axis=-1, keepdims=True)
    cen = v - mu
    var = jnp.mean(cen * cen, axis=-1, keepdims=True)
    vn = (cen * lax.rsqrt(var + EPS) * sgg_ref[...] + sgbt_ref[...]).astype(BF16)
    n_ch = tile // CHUNK
    lane = lax.broadcasted_iota(jnp.int32, (CHUNK, n_ch * LANES), 1) % LANES
    even_head = lane < (D_SGU // SGU_HEADS)
    g_sg = _silu(p_sg[:, 2 * D_SGU:3 * D_SGU])
    for k in range(D_SGU // LANES):
        cols = slice(LANES * k, LANES * (k + 1))
        rhs = jnp.concatenate([vn[c * CHUNK:(c + 1) * CHUNK, cols] for c in range(n_ch)], axis=1)
        s_even = jnp.dot(sw_ref[2 * k], rhs, preferred_element_type=F32)
        s_odd = jnp.dot(sw_ref[2 * k + 1], rhs, preferred_element_type=F32)
        sel = jnp.where(even_head, s_even, s_odd)
        for c in range(n_ch):
            rows = slice(c * CHUNK, (c + 1) * CHUNK)
            s_c = sel[:, c * LANES:(c + 1) * LANES] + sbias_ref[:, cols]
            yg_ref[rows, D_CONV + LANES * k:D_CONV + LANES * (k + 1)] = (
                u[rows, cols] * s_c * g_sg[rows, cols]).astype(BF16)


def _pass_b_kernel(mid_ref, yg_ref, x_ref, wo_ref, gate_ref, h0_ref, fg_ref,
                   xo_ref, hend_ref, hb_s, p_s, u_s, yl_s, carry_s,
                   *, tile, sub_len, pitch_u, final_norm):
    i = pl.program_id(0)
    wo = wo_ref.at[0]

    @pl.when(i == 0)
    def _():
        carry_s[...] = h0_ref[...]

    unit = SUBLANES * sub_len
    for base in range(tile - unit, -1, -unit):
        h_end, p_end = _scan_loop(
            _row_loader(mid_ref, M_AB, base), _row_loader(mid_ref, M_BB, base),
            _row_storer(hb_s), _row_storer(p_s), sub_len, reverse=True)
        c_in = _chain_states(h_end, p_end, carry_s, reverse=True)

        def y_step(j, q, base=base, c_in=c_in):
            rows = slice(SUBLANES * j, SUBLANES * (j + 1))
            cols = slice(q * LANES, (q + 1) * LANES)
            hf = mid_ref[base + SUBLANES * j:base + SUBLANES * (j + 1),
                         M_HF + q * LANES:M_HF + (q + 1) * LANES]
            return hf + hb_s[rows, cols] + p_s[rows, cols] * c_in[:, cols]

        def emit(s, q, block, base=base):
            rows = slice(base + s * sub_len, base + (s + 1) * sub_len)
            gl = yg_ref[rows, Y_GL + q * LANES:Y_GL + (q + 1) * LANES].astype(F32)
            yl_s[rows, q * LANES:(q + 1) * LANES] = (block * gl).astype(BF16)

        _to_natural(y_step, D_LRU // LANES, u_s, sub_len, pitch_u, emit)
    hend_ref[...] = carry_s[...]

    o = jnp.dot(yl_s[...], wo[0:D_LRU, :], preferred_element_type=F32)
    o = o + jnp.dot(yg_ref[:, Y_MIX:Y_MIX + D_LRU], wo[D_LRU:D_MODEL, :],
                    preferred_element_type=F32)
    xn = x_ref[...] + gate_ref[...] * o
    if final_norm:
        ms = jnp.mean(xn * xn, axis=-1, keepdims=True)
        xn = xn * lax.rsqrt(ms + EPS) * fg_ref[...]
    xo_ref[...] = xn


def _const_spec(shape):
    nd = len(shape)
    return pl.BlockSpec(shape, lambda i: (0,) * nd, pipeline_mode=pl.Buffered(1))


def _layer_spec(stacked, layer):
    shape = (1,) + stacked.shape[1:]
    nd = len(shape)
    return pl.BlockSpec(shape, lambda i: (layer,) + (0,) * (nd - 1), pipeline_mode=pl.Buffered(1))


def _mod_call(cvec, w_mod, b_mod):
    depth = w_mod.shape[0]
    bn = 768
    return pl.pallas_call(
        _mod_kernel,
        grid=(depth, 3 * D_MODEL // bn),
        in_specs=[
            pl.BlockSpec((SUBLANES, D_MODEL), lambda l, j: (0, 0)),
            pl.BlockSpec((1, D_MODEL, bn), lambda l, j: (l, 0, j)),
            pl.BlockSpec((1, 1, bn), lambda l, j: (l, 0, j)),
        ],
        out_specs=pl.BlockSpec((1, SUBLANES, bn), lambda l, j: (l, 0, j)),
        out_shape=jax.ShapeDtypeStruct((depth, SUBLANES, 3 * D_MODEL), F32),
        compiler_params=pltpu.CompilerParams(dimension_semantics=("arbitrary", "arbitrary")),
        name="mod_proj",
    )(cvec, w_mod, b_mod.reshape(depth, 1, 3 * D_MODEL))


def _tiling(rows):
    tile = min(TILE, rows)
    assert rows % tile == 0 and tile % (SUBLANES * SUBLANES) == 0 and tile % CHUNK == 0
    sub_len = tile // SUBLANES
    return tile, rows // tile, sub_len


def _pass_a_call(x, mod_row, h0, layer, w_in, p):
    rows = x.shape[0]
    tile, n, sub_len = _tiling(rows)
    pitch_l = _odd_pitch(sub_len + 2 * LRU_HALO)
    pitch_z = _odd_pitch(sub_len + 2 * HALO)
    pitch_u = _odd_pitch(sub_len)
    ext = tile + 2 * HALO
    hb = tile // HALO
    n_hb = rows // HALO
    row_spec = lambda w: pl.BlockSpec((tile, w), lambda i: (i, 0))
    consts = [p["lru_conv_w"], p["lru_conv_b"], p["lru_lam"],
              p["w_gate"], p["b_gate"], p["cv_w"], p["cv_b"], p["cv_ln_g"], p["cv_ln_b"],
              p["sgu_ln_g"], p["sgu_ln_b"], p["sgu_w"], p["sgu_bias"], h0]
    kern = functools.partial(_pass_a_kernel, tile=tile, sub_len=sub_len, pitch_l=pitch_l,
                             pitch_z=pitch_z, pitch_u=pitch_u)
    return pl.pallas_call(
        kern,
        grid=(n,),
        in_specs=[
            row_spec(D_MODEL),
            pl.BlockSpec((HALO, D_MODEL), lambda i: (jnp.minimum((i + 1) * hb, n_hb - 1), 0)),
            _const_spec(mod_row.shape), _const_spec(p["norm_g"].shape), _layer_spec(w_in, layer),
        ] + [_const_spec(c.shape) for c in consts],
        out_specs=[row_spec(D_MID), row_spec(D_YG), pl.BlockSpec((1, D_LRU), lambda i: (0, 0))],
        out_shape=[
            jax.ShapeDtypeStruct((rows, D_MID), F32),
            jax.ShapeDtypeStruct((rows, D_YG), BF16),
            jax.ShapeDtypeStruct((1, D_LRU), F32),
        ],
        scratch_shapes=[
            pltpu.VMEM((tile + HALO, D_MODEL), BF16),
            pltpu.VMEM((ext, D_EXT), F32),
            pltpu.VMEM((D_LRU // LANES, SUBLANES * pitch_l, LANES), F32),
            pltpu.VMEM((tile, D_LRU), F32),
            pltpu.VMEM((tile, D_LRU), F32),
            pltpu.VMEM((tile, D_LRU), F32),
            pltpu.VMEM((D_CONV // LANES, SUBLANES * pitch_z, LANES), F32),
            pltpu.VMEM((CONV_WIDTH, SUBLANES, D_CONV), F32),
            pltpu.VMEM((tile, D_CONV), F32),
            pltpu.VMEM((D_CONV // LANES, SUBLANES * pitch_u, LANES), F32),
            pltpu.VMEM((1, D_LRU), F32),
        ],
        compiler_params=pltpu.CompilerParams(
            dimension_semantics=("arbitrary",), vmem_limit_bytes=VMEM_LIMIT),
        name="pass_a",
    )(x, x, mod_row, p["norm_g"], w_in, *consts)


def _pass_b_call(mid, yg, x, gate_row, h0, layer, w_out, final_g, final_norm):
    rows = x.shape[0]
    unit, _, sub_len = _tiling(rows)
    tile = min(TILE_B, rows)
    assert rows % tile == 0 and tile % unit == 0
    n = rows // tile
    pitch_u = _odd_pitch(sub_len)
    row_spec = lambda w: pl.BlockSpec((tile, w), lambda i: (n - 1 - i, 0))
    consts = [gate_row, h0, final_g]
    kern = functools.partial(_pass_b_kernel, tile=tile, sub_len=sub_len, pitch_u=pitch_u,
                             final_norm=final_norm)
    return pl.pallas_call(
        kern,
        grid=(n,),
        in_specs=[row_spec(D_MID), row_spec(D_YG), row_spec(D_MODEL), _layer_spec(w_out, layer)]
        + [_const_spec(c.shape) for c in consts],
        out_specs=[row_spec(D_MODEL), pl.BlockSpec((1, D_LRU), lambda i: (0, 0))],
        out_shape=[jax.ShapeDtypeStruct((rows, D_MODEL), F32),
                   jax.ShapeDtypeStruct((1, D_LRU), F32)],
        scratch_shapes=[
            pltpu.VMEM((unit, D_LRU), F32),
            pltpu.VMEM((unit, D_LRU), F32),
            pltpu.VMEM((D_LRU // LANES, SUBLANES * pitch_u, LANES), F32),
            pltpu.VMEM((tile, D_LRU), BF16),
            pltpu.VMEM((1, D_LRU), F32),
        ],
        compiler_params=pltpu.CompilerParams(
            dimension_semantics=("arbitrary",), vmem_limit_bytes=VMEM_LIMIT),
        name="pass_b",
    )(mid, yg, x, w_out, *consts)


def _layer_params(l, norm_g, lru_conv_w, lru_conv_b, lru_lam, lru_w_r, lru_b_r, lru_w_i, lru_b_i,
                  cv_w, cv_b, cv_ln_g, cv_ln_b, sgu_ln_g, sgu_ln_b, sgu_w, sgu_b):
    gates = jnp.stack([lru_w_r[l, 0], lru_w_i[l, 0], lru_w_r[l, 1], lru_w_i[l, 1]])
    hpb = GATE_BLK // LRU_HEAD_DIM
    nb = D_LRU // GATE_BLK
    gates = gates.reshape(4, nb, hpb, LRU_HEAD_DIM, LRU_HEAD_DIM)
    bd = jnp.einsum("tjqio,qr->tjqiro", gates, jnp.eye(hpb, dtype=gates.dtype))
    bd = bd.reshape(4, nb, GATE_BLK, GATE_BLK).transpose(1, 2, 0, 3).reshape(nb, GATE_BLK, 4 * GATE_BLK)
    b_gate = jnp.stack([lru_b_r[l, 0], lru_b_i[l, 0], lru_b_r[l, 1], lru_b_i[l, 1]]).reshape(4, D_LRU)
    return dict(
        norm_g=norm_g[l].reshape(1, D_MODEL),
        lru_conv_w=lru_conv_w[l],
        lru_conv_b=lru_conv_b[l].reshape(1, D_LRU),
        lru_lam=lru_lam[l],
        w_gate=bd.astype(BF16),
        b_gate=b_gate,
        cv_w=cv_w[l],
        cv_b=cv_b[l].reshape(1, D_CONV),
        cv_ln_g=cv_ln_g[l].reshape(1, D_CONV),
        cv_ln_b=cv_ln_b[l].reshape(1, D_CONV),
        sgu_ln_g=sgu_ln_g[l].reshape(1, D_SGU),
        sgu_ln_b=sgu_ln_b[l].reshape(1, D_SGU),
        sgu_w=sgu_w[l].astype(BF16),
        sgu_bias=jnp.repeat(sgu_b[l].T, D_SGU // SGU_HEADS, axis=1),
    )


def kernel(x, c, ctx, c_ctx, w_mod, b_mod, norm_g, w_in, w_out, lru_conv_w, lru_conv_b, lru_lam,
           lru_w_r, lru_b_r, lru_w_i, lru_b_i, cv_w, cv_b, cv_ln_g, cv_ln_b, sgu_ln_g, sgu_ln_b,
           sgu_w, sgu_b, final_g):
    assert x.shape[0] == 1 and c.shape[0] == 1 and ctx.shape[0] == 1
    depth = w_mod.shape[0]
    xs = x[0]
    cs = ctx[0]
    cvec = jnp.concatenate(
        [c, c_ctx.reshape(1, D_MODEL), jnp.zeros((SUBLANES - 2, D_MODEL), F32)], axis=0)
    mod = _mod_call(cvec, w_mod, b_mod)
    fg = final_g.reshape(1, D_MODEL)
    zero_state = jnp.zeros((1, D_LRU), F32)
    w_in_bf = w_in.astype(BF16)
    w_out_bf = w_out.astype(BF16)
    for l in range(depth):
        last = l == depth - 1
        p = _layer_params(l, norm_g, lru_conv_w, lru_conv_b, lru_lam, lru_w_r, lru_b_r, lru_w_i,
                          lru_b_i, cv_w, cv_b, cv_ln_g, cv_ln_b, sgu_ln_g, sgu_ln_b, sgu_w, sgu_b)
        mod_x = mod[l, 0:1]
        mod_c = mod[l, 1:2]
        mid_c, yg_c, hf_end = _pass_a_call(cs, mod_c[:, 0:2 * D_MODEL], zero_state, l, w_in_bf, p)
        cs_new, hb_end = _pass_b_call(mid_c, yg_c, cs, mod_c[:, 2 * D_MODEL:], zero_state, l, w_out_bf,
                                      fg, False)
        mid_x, yg_x, _ = _pass_a_call(xs, mod_x[:, 0:2 * D_MODEL], hf_end, l, w_in_bf, p)
        xs, _ = _pass_b_call(mid_x, yg_x, xs, mod_x[:, 2 * D_MODEL:], hb_end, l, w_out_bf, fg, last)
        cs = cs_new
    return xs[None]
```

```python
import functools

import jax
import jax.numpy as jnp
from jax import lax
from jax.experimental import pallas as pl
from jax.experimental.pallas import tpu as pltpu

F32 = jnp.float32
BF16 = jnp.bfloat16

D_MODEL = 2048
EPS = 1e-6
D_LRU = 1024
LRU_HEADS = 16
LRU_HEAD_DIM = D_LRU // LRU_HEADS
LRU_C = 8.0
LRU_CONV = 4
D_CONV = 512
CONV_WIDTH = 31
D_SGU = 512
SGU_HEADS = 8
CHUNK = 128
D_IN = 5120
D_EXT = D_LRU + 2 * D_CONV
O_LRU_G = D_LRU
O_CV = 2 * D_LRU
O_CV_G = O_CV + 2 * D_CONV
O_SGU = O_CV_G + D_CONV
LOG2_E = 1.4426950408889634
M_HF, M_AB, M_BB, M_GL, D_MID = 0, D_LRU, 2 * D_LRU, 3 * D_LRU, 4 * D_LRU

LANES = 128
SUBLANES = 8
HALO = 16
LRU_HALO = 8
TILE = 256
TILE_B = 512
GATE_BLK = 256
VMEM_LIMIT = 56 * 1024 * 1024


def _sigmoid(v):
    return 1.0 / (1.0 + jnp.exp2(v * (-LOG2_E)))


def _silu(v):
    hv = 0.5 * v
    return hv + hv * jnp.tanh(hv)


def _odd_pitch(rows):
    return rows if (rows // SUBLANES) % 2 == 1 else rows + SUBLANES


def _replicate_subchunks(src_ref, col0, n_slab, dst_s, row0, rows, sub_len, pitch):
    for q in range(n_slab):
        cols = slice(col0 + q * LANES, col0 + (q + 1) * LANES)
        for s in range(SUBLANES):
            r = row0 + s * sub_len
            dst_s[q, s * pitch:s * pitch + rows, :] = src_ref[r:r + rows, cols]


def _interleaved_conv(src_s, n_slab, pitch, first, taps, weight, bias, out_s, sub_len):
    for q in range(n_slab):
        cols = slice(q * LANES, (q + 1) * LANES)
        acc = [bias(q)] * sub_len
        for m in range(sub_len + taps - 1):
            v = src_s[q, pl.ds(first + m, SUBLANES, stride=pitch), :]
            for j in range(sub_len):
                k = m - j
                if 0 <= k < taps:
                    acc[j] = acc[j] + weight(k, q) * v
        for j in range(sub_len):
            out_s[SUBLANES * j:SUBLANES * (j + 1), cols] = acc[j]


def _to_natural(val_fn, n_slab, u_s, sub_len, pitch, emit):
    for q in range(n_slab):
        for j in range(sub_len):
            u_s[q, pl.ds(j, SUBLANES, stride=pitch), :] = val_fn(j, q)
        for s in range(SUBLANES):
            emit(s, q, u_s[q, s * pitch:s * pitch + sub_len, :])


def _scan_loop(a_load, b_load, h_store, p_store, sub_len, reverse):
    n_slab = D_LRU // LANES

    hs = [jnp.zeros((SUBLANES, LANES), F32)] * n_slab
    ps = [jnp.ones((SUBLANES, LANES), F32)] * n_slab
    for jj in range(sub_len):
        r = SUBLANES * ((sub_len - 1 - jj) if reverse else jj)
        for q in range(n_slab):
            cols = slice(q * LANES, (q + 1) * LANES)
            a = a_load(r, cols)
            hs[q] = a * hs[q] + b_load(r, cols)
            ps[q] = a * ps[q]
            h_store(r, cols, hs[q])
            p_store(r, cols, ps[q])
    return jnp.concatenate(hs, axis=1), jnp.concatenate(ps, axis=1)


def _chain_states(h_end, p_end, carry_ref, reverse):
    c = carry_ref[...]
    rows = [None] * SUBLANES
    order = range(SUBLANES - 1, -1, -1) if reverse else range(SUBLANES)
    for s in order:
        rows[s] = c
        c = h_end[s:s + 1, :] + p_end[s:s + 1, :] * c
    carry_ref[...] = c
    return jnp.concatenate(rows, axis=0)


def _row_loader(ref, col0=0, row0=0):
    return lambda r, cols: ref[pl.ds(row0 + r, SUBLANES), slice(col0 + cols.start, col0 + cols.stop)]


def _row_storer(ref):
    def store(r, cols, v):
        ref[pl.ds(r, SUBLANES), cols] = v
    return store


def _mod_kernel(c_ref, w_ref, b_ref, o_ref):
    act = _silu(c_ref[...]).astype(BF16)
    o_ref[0] = jnp.dot(act, w_ref[0].astype(BF16), preferred_element_type=F32) + b_ref[0]


def _pass_a_kernel(x_ref, xn_ref, mod_ref, ng_ref, w_ref, lcw_ref, lcb_ref, lam_ref,
                   wg_ref, bg_ref, cvw_ref, cvb_ref, cvg_ref, cvbt_ref, sgg_ref, sgbt_ref,
                   sw_ref, sbias_ref, h0_ref,
                   mid_ref, yg_ref, hend_ref,
                   hn_s, pext_s, cl_s, xl_s, a_s, b_s, cz_s, w8_s, yc_s, u_s, carry_s,
                   *, tile, sub_len, pitch_l, pitch_z, pitch_u):
    i = pl.program_id(0)
    n = pl.num_programs(0)
    ext = tile + 2 * HALO
    w = w_ref.at[0]

    @pl.when(i == 0)
    def _():
        carry_s[...] = h0_ref[...]
        pext_s[tile:tile + HALO, :] = jnp.zeros((HALO, D_EXT), F32)
        for k in range(CONV_WIDTH):
            w8_s[k] = jnp.broadcast_to(cvw_ref[k:k + 1, :], (SUBLANES, D_CONV))

    pext_s[0:HALO, :] = pext_s[tile:tile + HALO, :]

    shift = mod_ref[:, 0:D_MODEL]
    gain = ng_ref[...] * (1.0 + mod_ref[:, D_MODEL:2 * D_MODEL])

    def norm_mod(v):
        ms = jnp.mean(v * v, axis=-1, keepdims=True)
        return v * lax.rsqrt(ms + EPS) * gain + shift

    keep_next = jnp.where(i == n - 1, 0.0, 1.0).astype(F32)
    hn_s[0:tile, :] = norm_mod(x_ref[...]).astype(BF16)
    hn_s[tile:tile + HALO, :] = (norm_mod(xn_ref[...]) * keep_next).astype(BF16)

    pext_s[HALO:ext, 0:D_LRU] = jnp.dot(hn_s[...], w[:, 0:D_LRU], preferred_element_type=F32)
    pext_s[HALO:ext, D_LRU:D_EXT] = jnp.dot(hn_s[...], w[:, O_CV:O_CV_G],
                                            preferred_element_type=F32)

    n_slab = D_LRU // LANES
    _replicate_subchunks(pext_s, 0, n_slab, cl_s, HALO - LRU_HALO, sub_len + 2 * LRU_HALO,
                         sub_len, pitch_l)
    lru_taps = {}

    def lru_weight(k, q):
        if (k, q) not in lru_taps:
            lru_taps[k, q] = jnp.broadcast_to(
                lcw_ref[k:k + 1, q * LANES:(q + 1) * LANES], (SUBLANES, LANES))
        return lru_taps[k, q]

    _interleaved_conv(
        cl_s, n_slab, pitch_l, LRU_HALO - LRU_CONV // 2, LRU_CONV, lru_weight,
        lambda q: jnp.broadcast_to(lcb_ref[:, q * LANES:(q + 1) * LANES], (SUBLANES, LANES)),
        xl_s, sub_len)

    hn_main = hn_s.at[0:tile, :]
    mid_ref[:, M_GL:M_GL + D_LRU] = _silu(
        jnp.dot(hn_main[...], w[:, O_LRU_G:O_CV], preferred_element_type=F32))

    neg_lam = -lam_ref[...]
    softplus = jnp.maximum(neg_lam, 0.0) + jnp.log1p(jnp.exp(-jnp.abs(neg_lam)))
    decay = LRU_C * softplus
    decay_log2 = decay * (-LOG2_E)
    for j in range(D_LRU // GATE_BLK):
        cs = slice(GATE_BLK * j, GATE_BLK * (j + 1))
        xl = xl_s[:, cs]
        g = jnp.dot(xl.astype(BF16), wg_ref[j], preferred_element_type=F32)
        for d in range(2):
            g0 = 2 * GATE_BLK * d
            r = _sigmoid(g[:, g0:g0 + GATE_BLK] + bg_ref[2 * d:2 * d + 1, cs])
            gi = _sigmoid(g[:, g0 + GATE_BLK:g0 + 2 * GATE_BLK] + bg_ref[2 * d + 1:2 * d + 2, cs])
            neg_log_a = decay[d:d + 1, cs] * r
            a = jnp.exp2(decay_log2[d:d + 1, cs] * r)
            sq = jnp.tanh(neg_log_a) * (a * a + 1.0)
            root = jnp.where(sq > 0.0, sq * lax.rsqrt(sq), 0.0)
            bv = root * (gi * xl)
            if d == 0:
                a_s[:, cs] = a
                b_s[:, cs] = bv
            else:
                mid_ref[:, M_AB + GATE_BLK * j:M_AB + GATE_BLK * (j + 1)] = a
                mid_ref[:, M_BB + GATE_BLK * j:M_BB + GATE_BLK * (j + 1)] = bv

    h_end, p_end = _scan_loop(_row_loader(a_s), _row_loader(b_s), _row_storer(b_s), _row_storer(a_s),
                              sub_len, reverse=False)
    c_in = _chain_states(h_end, p_end, carry_s, reverse=False)
    hend_ref[...] = carry_s[...]
    for j in range(sub_len):
        rows = slice(SUBLANES * j, SUBLANES * (j + 1))
        mid_ref[rows, M_HF:M_HF + D_LRU] = b_s[rows, :] + a_s[rows, :] * c_in

    n_slab = D_CONV // LANES
    for q in range(n_slab):
        ca = slice(D_LRU + q * LANES, D_LRU + (q + 1) * LANES)
        cg = slice(D_LRU + D_CONV + q * LANES, D_LRU + D_CONV + (q + 1) * LANES)
        for s in range(SUBLANES):
            r = s * sub_len
            rows = sub_len + 2 * HALO
            cz_s[q, s * pitch_z:s * pitch_z + rows, :] = (
                pext_s[r:r + rows, ca] * _sigmoid(pext_s[r:r + rows, cg]))
    _interleaved_conv(
        cz_s, n_slab, pitch_z, HALO - CONV_WIDTH // 2, CONV_WIDTH,
        lambda k, q: w8_s[k, :, q * LANES:(q + 1) * LANES],
        lambda q: jnp.broadcast_to(cvb_ref[:, q * LANES:(q + 1) * LANES], (SUBLANES, LANES)),
        yc_s, sub_len)
    acc = yc_s[...]
    mu = jnp.mean(acc, axis=-1, keepdims=True)
    cen = acc - mu
    var = jnp.mean(cen * cen, axis=-1, keepdims=True)
    yc_s[...] = _silu(cen * lax.rsqrt(var + EPS) * cvg_ref[...] + cvbt_ref[...])
    g_cv = _silu(jnp.dot(hn_main[...], w[:, O_CV_G:O_SGU], preferred_element_type=F32))

    def emit_cv(s, q, block):
        rows = slice(s * sub_len, (s + 1) * sub_len)
        cols = slice(q * LANES, (q + 1) * LANES)
        yg_ref[rows, cols] = (block * g_cv[rows, cols]).astype(BF16)

    _to_natural(lambda j, q: yc_s[SUBLANES * j:SUBLANES * (j + 1), q * LANES:(q + 1) * LANES],
                n_slab, u_s, sub_len, pitch_u, emit_cv)

    p_sg = jnp.dot(hn_main[...], w[:, O_SGU:D_IN], preferred_element_type=F32)
    zz = jax.nn.gelu(p_sg[:, 0:2 * D_SGU])
    u = zz[:, 0:D_SGU]
    v = zz[:, D_SGU:2 * D_SGU]
    mu = jnp.mean(v, axis=-1, keepdims=True)
    cen = v - mu
    var = jnp.mean(cen * cen, axis=-1, keepdims=True)
    vn = (cen * lax.rsqrt(var + EPS) * sgg_ref[...] + sgbt_ref[...]).astype(BF16)
    n_ch = tile // CHUNK
    lane = lax.broadcasted_iota(jnp.int32, (CHUNK, n_ch * LANES), 1) % LANES
    even_head = lane < (D_SGU // SGU_HEADS)
    g_sg = _silu(p_sg[:, 2 * D_SGU:3 * D_SGU])
    for k in range(D_SGU // LANES):
        cols = slice(LANES * k, LANES * (k + 1))
        rhs = jnp.concatenate([vn[c * CHUNK:(c + 1) * CHUNK, cols] for c in range(n_ch)], axis=1)
        s_even = jnp.dot(sw_ref[2 * k], rhs, preferred_element_type=F32)
        s_odd = jnp.dot(sw_ref[2 * k + 1], rhs, preferred_element_type=F32)
        sel = jnp.where(even_head, s_even, s_odd)
        for c in range(n_ch):
            rows = slice(c * CHUNK, (c + 1) * CHUNK)
            s_c = sel[:, c * LANES:(c + 1) * LANES] + sbias_ref[:, cols]
            yg_ref[rows, D_CONV + LANES * k:D_CONV + LANES * (k + 1)] = (
                u[rows, cols] * s_c * g_sg[rows, cols]).astype(BF16)


def _pass_a_cast_kernel(*refs, n_in, n_out, **dims):
    ins, outs, scratch = refs[:n_in], refs[n_in:n_in + n_out], refs[n_in + n_out:]
    *a_ins, wi_src, wo_src = ins
    *a_outs, wi_dst, wo_dst = outs
    wi_dst[...] = wi_src[...].astype(BF16)
    wo_dst[...] = wo_src[...].astype(BF16)
    _pass_a_kernel(*a_ins, *a_outs, *scratch, **dims)


def _pass_b_kernel(mid_ref, yg_ref, x_ref, wo_ref, gate_ref, h0_ref, fg_ref,
                   xo_ref, hend_ref, hb_s, p_s, u_s, yl_s, carry_s,
                   *, tile, sub_len, pitch_u, final_norm):
    i = pl.program_id(0)
    wo = wo_ref.at[0]

    @pl.when(i == 0)
    def _():
        carry_s[...] = h0_ref[...]

    unit = SUBLANES * sub_len
    for base in range(tile - unit, -1, -unit):
        h_end, p_end = _scan_loop(
            _row_loader(mid_ref, M_AB, base), _row_loader(mid_ref, M_BB, base),
            _row_storer(hb_s), _row_storer(p_s), sub_len, reverse=True)
        c_in = _chain_states(h_end, p_end, carry_s, reverse=True)

        def y_step(j, q, base=base, c_in=c_in):
            rows = slice(SUBLANES * j, SUBLANES * (j + 1))
            cols = slice(q * LANES, (q + 1) * LANES)
            hf = mid_ref[base + SUBLANES * j:base + SUBLANES * (j + 1),
                         M_HF + q * LANES:M_HF + (q + 1) * LANES]
            return hf + hb_s[rows, cols] + p_s[rows, cols] * c_in[:, cols]

        def emit(s, q, block, base=base):
            rows = slice(base + s * sub_len, base + (s + 1) * sub_len)
            gl = mid_ref[rows, M_GL + q * LANES:M_GL + (q + 1) * LANES]
            yl_s[rows, q * LANES:(q + 1) * LANES] = (block * gl).astype(BF16)

        _to_natural(y_step, D_LRU // LANES, u_s, sub_len, pitch_u, emit)
    hend_ref[...] = carry_s[...]

    o = jnp.dot(yl_s[...], wo[0:D_LRU, :], preferred_element_type=F32)
    o = o + jnp.dot(yg_ref[...], wo[D_LRU:D_MODEL, :], preferred_element_type=F32)
    xn = x_ref[...] + gate_ref[...] * o
    if final_norm:
        ms = jnp.mean(xn * xn, axis=-1, keepdims=True)
        xn = xn * lax.rsqrt(ms + EPS) * fg_ref[...]
    xo_ref[...] = xn


def _const_spec(shape):
    nd = len(shape)
    return pl.BlockSpec(shape, lambda i: (0,) * nd, pipeline_mode=pl.Buffered(1))


def _mod_call(cvec, w_mod, b_mod):
    depth = w_mod.shape[0]
    bn = 768
    return pl.pallas_call(
        _mod_kernel,
        grid=(depth, 3 * D_MODEL // bn),
        in_specs=[
            pl.BlockSpec((SUBLANES, D_MODEL), lambda l, j: (0, 0)),
            pl.BlockSpec((1, D_MODEL, bn), lambda l, j: (l, 0, j)),
            pl.BlockSpec((1, 1, bn), lambda l, j: (l, 0, j)),
        ],
        out_specs=pl.BlockSpec((1, SUBLANES, bn), lambda l, j: (l, 0, j)),
        out_shape=jax.ShapeDtypeStruct((depth, SUBLANES, 3 * D_MODEL), F32),
        compiler_params=pltpu.CompilerParams(dimension_semantics=("arbitrary", "arbitrary")),
        name="mod_proj",
    )(cvec, w_mod, b_mod.reshape(depth, 1, 3 * D_MODEL))


def _tiling(rows):
    tile = min(TILE, rows)
    assert rows % tile == 0 and tile % (SUBLANES * SUBLANES) == 0 and tile % CHUNK == 0
    sub_len = tile // SUBLANES
    return tile, rows // tile, sub_len


def _pass_a_call(x, mod_row, h0, w_in, p, cast_next=None):
    rows = x.shape[0]
    tile, n, sub_len = _tiling(rows)
    pitch_l = _odd_pitch(sub_len + 2 * LRU_HALO)
    pitch_z = _odd_pitch(sub_len + 2 * HALO)
    pitch_u = _odd_pitch(sub_len)
    ext = tile + 2 * HALO
    hb = tile // HALO
    n_hb = rows // HALO
    row_spec = lambda w: pl.BlockSpec((tile, w), lambda i: (i, 0))
    consts = [p["lru_conv_w"], p["lru_conv_b"], p["lru_lam"],
              p["w_gate"], p["b_gate"], p["cv_w"], p["cv_b"], p["cv_ln_g"], p["cv_ln_b"],
              p["sgu_ln_g"], p["sgu_ln_b"], p["sgu_w"], p["sgu_bias"], h0]
    dims = dict(tile=tile, sub_len=sub_len, pitch_l=pitch_l, pitch_z=pitch_z, pitch_u=pitch_u)
    operands = [x, x, mod_row, p["norm_g"], w_in, *consts]
    in_specs = [
        row_spec(D_MODEL),
        pl.BlockSpec((HALO, D_MODEL), lambda i: (jnp.minimum((i + 1) * hb, n_hb - 1), 0)),
        _const_spec(mod_row.shape), _const_spec(p["norm_g"].shape), _const_spec(w_in.shape),
    ] + [_const_spec(c.shape) for c in consts]
    out_specs = [row_spec(D_MID), row_spec(D_LRU), pl.BlockSpec((1, D_LRU), lambda i: (0, 0))]
    out_shape = [
        jax.ShapeDtypeStruct((rows, D_MID), F32),
        jax.ShapeDtypeStruct((rows, D_LRU), BF16),
        jax.ShapeDtypeStruct((1, D_LRU), F32),
    ]
    kern = functools.partial(_pass_a_kernel, **dims)
    if cast_next is not None:
        *stacks, nxt = cast_next
        for wgt in stacks:
            k_rows, k_cols = wgt.shape[1:]
            assert k_rows % n == 0 and (k_rows // n) % (2 * SUBLANES) == 0
            chunk = pl.BlockSpec((1, k_rows // n, k_cols), lambda i: (nxt, i, 0))
            operands.append(wgt)
            in_specs.append(chunk)
            out_specs.append(pl.BlockSpec((1, k_rows // n, k_cols), lambda i: (0, i, 0)))
            out_shape.append(jax.ShapeDtypeStruct((1, k_rows, k_cols), BF16))
        kern = functools.partial(_pass_a_cast_kernel, n_in=len(in_specs), n_out=len(out_specs),
                                 **dims)
    return pl.pallas_call(
        kern,
        grid=(n,),
        in_specs=in_specs,
        out_specs=out_specs,
        out_shape=out_shape,
        scratch_shapes=[
            pltpu.VMEM((tile + HALO, D_MODEL), BF16),
            pltpu.VMEM((ext, D_EXT), F32),
            pltpu.VMEM((D_LRU // LANES, SUBLANES * pitch_l, LANES), F32),
            pltpu.VMEM((tile, D_LRU), F32),
            pltpu.VMEM((tile, D_LRU), F32),
            pltpu.VMEM((tile, D_LRU), F32),
            pltpu.VMEM((D_CONV // LANES, SUBLANES * pitch_z, LANES), F32),
            pltpu.VMEM((CONV_WIDTH, SUBLANES, D_CONV), F32),
            pltpu.VMEM((tile, D_CONV), F32),
            pltpu.VMEM((D_CONV // LANES, SUBLANES * pitch_u, LANES), F32),
            pltpu.VMEM((1, D_LRU), F32),
        ],
        compiler_params=pltpu.CompilerParams(
            dimension_semantics=("arbitrary",), vmem_limit_bytes=VMEM_LIMIT),
        name="pass_a",
    )(*operands)


def _pass_b_call(mid, yg, x, gate_row, h0, w_out, final_g, final_norm):
    rows = x.shape[0]
    unit, _, sub_len = _tiling(rows)
    tile = min(TILE_B, rows)
    assert rows % tile == 0 and tile % unit == 0
    n = rows // tile
    pitch_u = _odd_pitch(sub_len)
    row_spec = lambda w: pl.BlockSpec((tile, w), lambda i: (n - 1 - i, 0))
    consts = [gate_row, h0, final_g]
    kern = functools.partial(_pass_b_kernel, tile=tile, sub_len=sub_len, pitch_u=pitch_u,
                             final_norm=final_norm)
    return pl.pallas_call(
        kern,
        grid=(n,),
        in_specs=[row_spec(D_MID), row_spec(D_LRU), row_spec(D_MODEL), _const_spec(w_out.shape)]
        + [_const_spec(c.shape) for c in consts],
        out_specs=[row_spec(D_MODEL), pl.BlockSpec((1, D_LRU), lambda i: (0, 0))],
        out_shape=[jax.ShapeDtypeStruct((rows, D_MODEL), F32),
                   jax.ShapeDtypeStruct((1, D_LRU), F32)],
        scratch_shapes=[
            pltpu.VMEM((unit, D_LRU), F32),
            pltpu.VMEM((unit, D_LRU), F32),
            pltpu.VMEM((D_LRU // LANES, SUBLANES * pitch_u, LANES), F32),
            pltpu.VMEM((tile, D_LRU), BF16),
            pltpu.VMEM((1, D_LRU), F32),
        ],
        compiler_params=pltpu.CompilerParams(
            dimension_semantics=("arbitrary",), vmem_limit_bytes=VMEM_LIMIT),
        name="pass_b",
    )(mid, yg, x, w_out, *consts)


def _layer_params(l, norm_g, lru_conv_w, lru_conv_b, lru_lam, lru_w_r, lru_b_r, lru_w_i, lru_b_i,
                  cv_w, cv_b, cv_ln_g, cv_ln_b, sgu_ln_g, sgu_ln_b, sgu_w, sgu_b):
    gates = jnp.stack([lru_w_r[l, 0], lru_w_i[l, 0], lru_w_r[l, 1], lru_w_i[l, 1]])
    hpb = GATE_BLK // LRU_HEAD_DIM
    nb = D_LRU // GATE_BLK
    gates = gates.reshape(4, nb, hpb, LRU_HEAD_DIM, LRU_HEAD_DIM)
    bd = jnp.einsum("tjqio,qr->tjqiro", gates, jnp.eye(hpb, dtype=gates.dtype))
    bd = bd.reshape(4, nb, GATE_BLK, GATE_BLK).transpose(1, 2, 0, 3).reshape(nb, GATE_BLK, 4 * GATE_BLK)
    b_gate = jnp.stack([lru_b_r[l, 0], lru_b_i[l, 0], lru_b_r[l, 1], lru_b_i[l, 1]]).reshape(4, D_LRU)
    return dict(
        norm_g=norm_g[l].reshape(1, D_MODEL),
        lru_conv_w=lru_conv_w[l],
        lru_conv_b=lru_conv_b[l].reshape(1, D_LRU),
        lru_lam=lru_lam[l],
        w_gate=bd.astype(BF16),
        b_gate=b_gate,
        cv_w=cv_w[l],
        cv_b=cv_b[l].reshape(1, D_CONV),
        cv_ln_g=cv_ln_g[l].reshape(1, D_CONV),
        cv_ln_b=cv_ln_b[l].reshape(1, D_CONV),
        sgu_ln_g=sgu_ln_g[l].reshape(1, D_SGU),
        sgu_ln_b=sgu_ln_b[l].reshape(1, D_SGU),
        sgu_w=sgu_w[l].astype(BF16),
        sgu_bias=jnp.repeat(sgu_b[l].T, D_SGU // SGU_HEADS, axis=1),
    )


def kernel(x, c, ctx, c_ctx, w_mod, b_mod, norm_g, w_in, w_out, lru_conv_w, lru_conv_b, lru_lam,
           lru_w_r, lru_b_r, lru_w_i, lru_b_i, cv_w, cv_b, cv_ln_g, cv_ln_b, sgu_ln_g, sgu_ln_b,
           sgu_w, sgu_b, final_g):
    assert x.shape[0] == 1 and c.shape[0] == 1 and ctx.shape[0] == 1
    depth = w_mod.shape[0]
    xs = x[0]
    cs = ctx[0]
    cvec = jnp.concatenate(
        [c, c_ctx.reshape(1, D_MODEL), jnp.zeros((SUBLANES - 2, D_MODEL), F32)], axis=0)
    mod = _mod_call(cvec, w_mod, b_mod)
    fg = final_g.reshape(1, D_MODEL)
    zero_state = jnp.zeros((1, D_LRU), F32)
    w_in_bf = w_in[0:1].astype(BF16)
    w_out_bf = w_out[0:1].astype(BF16)
    for l in range(depth):
        last = l == depth - 1
        p = _layer_params(l, norm_g, lru_conv_w, lru_conv_b, lru_lam, lru_w_r, lru_b_r, lru_w_i,
                          lru_b_i, cv_w, cv_b, cv_ln_g, cv_ln_b, sgu_ln_g, sgu_ln_b, sgu_w, sgu_b)
        mod_x = mod[l, 0:1]
        mod_c = mod[l, 1:2]
        mid_c, yg_c, hf_end = _pass_a_call(cs, mod_c[:, 0:2 * D_MODEL], zero_state, w_in_bf, p)
        cs_new, hb_end = _pass_b_call(mid_c, yg_c, cs, mod_c[:, 2 * D_MODEL:], zero_state, w_out_bf,
                                      fg, False)
        mid_x, yg_x, _, *next_bf = _pass_a_call(
            xs, mod_x[:, 0:2 * D_MODEL], hf_end, w_in_bf, p,
            cast_next=None if last else (w_in, w_out, l + 1))
        xs, _ = _pass_b_call(mid_x, yg_x, xs, mod_x[:, 2 * D_MODEL:], hb_end, w_out_bf, fg, last)
        cs = cs_new
        if not last:
            w_in_bf, w_out_bf = next_bf
    return xs[None]
```

```python
import functools

import jax
import jax.numpy as jnp
from jax import lax
from jax.experimental import pallas as pl
from jax.experimental.pallas import tpu as pltpu

F32 = jnp.float32
BF16 = jnp.bfloat16

D_MODEL = 2048
EPS = 1e-6
D_LRU = 1024
LRU_HEADS = 16
LRU_HEAD_DIM = D_LRU // LRU_HEADS
LRU_C = 8.0
LRU_CONV = 4
D_CONV = 512
CONV_WIDTH = 31
D_SGU = 512
SGU_HEADS = 8
CHUNK = 128
D_IN = 5120
D_EXT = D_LRU + 2 * D_CONV
O_LRU_G = D_LRU
O_CV = 2 * D_LRU
O_CV_G = O_CV + 2 * D_CONV
O_SGU = O_CV_G + D_CONV
LOG2_E = 1.4426950408889634
M_HF, M_AB, M_BB, M_GL, D_MID = 0, D_LRU, 2 * D_LRU, 3 * D_LRU, 4 * D_LRU

LANES = 128
SUBLANES = 8
HALO = 16
LRU_HALO = 8
TILE = 256
TILE_B = 512
GATE_BLK = 256
RING_SLOTS = 3
VMEM_LIMIT = 56 * 1024 * 1024
VMEM_LIMIT_B = 62 * 1024 * 1024


def _sigmoid(v):
    return 1.0 / (1.0 + jnp.exp2(v * (-LOG2_E)))


def _silu(v):
    hv = 0.5 * v
    return hv + hv * jnp.tanh(hv)


def _odd_pitch(rows):
    return rows if (rows // SUBLANES) % 2 == 1 else rows + SUBLANES


def _replicate_subchunks(src_ref, col0, n_slab, dst_s, row0, rows, sub_len, pitch):
    for q in range(n_slab):
        cols = slice(col0 + q * LANES, col0 + (q + 1) * LANES)
        for s in range(SUBLANES):
            r = row0 + s * sub_len
            dst_s[q, s * pitch:s * pitch + rows, :] = src_ref[r:r + rows, cols]


def _interleaved_conv(src_s, n_slab, pitch, first, taps, weight, bias, out_s, sub_len):
    for q in range(n_slab):
        cols = slice(q * LANES, (q + 1) * LANES)
        acc = [bias(q)] * sub_len
        for m in range(sub_len + taps - 1):
            v = src_s[q, pl.ds(first + m, SUBLANES, stride=pitch), :]
            for j in range(sub_len):
                k = m - j
                if 0 <= k < taps:
                    acc[j] = acc[j] + weight(k, q) * v
        for j in range(sub_len):
            out_s[SUBLANES * j:SUBLANES * (j + 1), cols] = acc[j]


def _to_natural(val_fn, n_slab, u_s, sub_len, pitch, emit):
    for q in range(n_slab):
        for j in range(sub_len):
            u_s[q, pl.ds(j, SUBLANES, stride=pitch), :] = val_fn(j, q)
        for s in range(SUBLANES):
            emit(s, q, u_s[q, s * pitch:s * pitch + sub_len, :])


def _scan_loop(a_load, b_load, h_store, p_store, sub_len, reverse):
    n_slab = D_LRU // LANES

    hs = [jnp.zeros((SUBLANES, LANES), F32)] * n_slab
    ps = [jnp.ones((SUBLANES, LANES), F32)] * n_slab
    for jj in range(sub_len):
        r = SUBLANES * ((sub_len - 1 - jj) if reverse else jj)
        for q in range(n_slab):
            cols = slice(q * LANES, (q + 1) * LANES)
            a = a_load(r, cols)
            hs[q] = a * hs[q] + b_load(r, cols)
            ps[q] = a * ps[q]
            h_store(r, cols, hs[q])
            p_store(r, cols, ps[q])
    return jnp.concatenate(hs, axis=1), jnp.concatenate(ps, axis=1)


def _chain_states(h_end, p_end, carry_ref, reverse):
    c = carry_ref[...]
    rows = [None] * SUBLANES
    order = range(SUBLANES - 1, -1, -1) if reverse else range(SUBLANES)
    for s in order:
        rows[s] = c
        c = h_end[s:s + 1, :] + p_end[s:s + 1, :] * c
    carry_ref[...] = c
    return jnp.concatenate(rows, axis=0)


def _row_loader(ref, col0=0, row0=0):
    return lambda r, cols: ref[pl.ds(row0 + r, SUBLANES), slice(col0 + cols.start, col0 + cols.stop)]


def _row_storer(ref):
    def store(r, cols, v):
        ref[pl.ds(r, SUBLANES), cols] = v
    return store


def _mod_kernel(c_ref, w_ref, b_ref, o_ref):
    act = _silu(c_ref[...]).astype(BF16)
    o_ref[0] = jnp.dot(act, w_ref[0].astype(BF16), preferred_element_type=F32) + b_ref[0]


def _pass_a_kernel(x_ref, xn_ref, mod_ref, ng_ref, w_ref, lcw_ref, lcb_ref, lam_ref,
                   wg_ref, bg_ref, cvw_ref, cvb_ref, cvg_ref, cvbt_ref, sgg_ref, sgbt_ref,
                   sw_ref, sbias_ref, h0_ref,
                   mid_ref, yg_ref, hend_ref,
                   hn_s, pext_s, cl_s, xl_s, a_s, b_s, cz_s, w8_s, yc_s, u_s, carry_s,
                   *, tile, sub_len, pitch_l, pitch_z, pitch_u):
    i = pl.program_id(0)
    n = pl.num_programs(0)
    ext = tile + 2 * HALO
    w = w_ref.at[0]

    @pl.when(i == 0)
    def _():
        carry_s[...] = h0_ref[...]
        pext_s[tile:tile + HALO, :] = jnp.zeros((HALO, D_EXT), F32)
        for k in range(CONV_WIDTH):
            w8_s[k] = jnp.broadcast_to(cvw_ref[k:k + 1, :], (SUBLANES, D_CONV))

    pext_s[0:HALO, :] = pext_s[tile:tile + HALO, :]

    shift = mod_ref[:, 0:D_MODEL]
    gain = ng_ref[...] * (1.0 + mod_ref[:, D_MODEL:2 * D_MODEL])

    def norm_mod(v):
        ms = jnp.mean(v * v, axis=-1, keepdims=True)
        return v * lax.rsqrt(ms + EPS) * gain + shift

    keep_next = jnp.where(i == n - 1, 0.0, 1.0).astype(F32)
    hn_s[0:tile, :] = norm_mod(x_ref[...]).astype(BF16)
    hn_s[tile:tile + HALO, :] = (norm_mod(xn_ref[...]) * keep_next).astype(BF16)

    pext_s[HALO:ext, 0:D_LRU] = jnp.dot(hn_s[...], w[:, 0:D_LRU], preferred_element_type=F32)
    pext_s[HALO:ext, D_LRU:D_EXT] = jnp.dot(hn_s[...], w[:, O_CV:O_CV_G],
                                            preferred_element_type=F32)

    n_slab = D_LRU // LANES
    _replicate_subchunks(pext_s, 0, n_slab, cl_s, HALO - LRU_HALO, sub_len + 2 * LRU_HALO,
                         sub_len, pitch_l)
    lru_taps = {}

    def lru_weight(k, q):
        if (k, q) not in lru_taps:
            lru_taps[k, q] = jnp.broadcast_to(
                lcw_ref[k:k + 1, q * LANES:(q + 1) * LANES], (SUBLANES, LANES))
        return lru_taps[k, q]

    _interleaved_conv(
        cl_s, n_slab, pitch_l, LRU_HALO - LRU_CONV // 2, LRU_CONV, lru_weight,
        lambda q: jnp.broadcast_to(lcb_ref[:, q * LANES:(q + 1) * LANES], (SUBLANES, LANES)),
        xl_s, sub_len)

    hn_main = hn_s.at[0:tile, :]
    mid_ref[:, M_GL:M_GL + D_LRU] = _silu(
        jnp.dot(hn_main[...], w[:, O_LRU_G:O_CV], preferred_element_type=F32))

    neg_lam = -lam_ref[...]
    softplus = jnp.maximum(neg_lam, 0.0) + jnp.log1p(jnp.exp(-jnp.abs(neg_lam)))
    decay = LRU_C * softplus
    decay_log2 = decay * (-LOG2_E)
    for j in range(D_LRU // GATE_BLK):
        cs = slice(GATE_BLK * j, GATE_BLK * (j + 1))
        xl = xl_s[:, cs]
        g = jnp.dot(xl.astype(BF16), wg_ref[j], preferred_element_type=F32)
        for d in range(2):
            g0 = 2 * GATE_BLK * d
            r = _sigmoid(g[:, g0:g0 + GATE_BLK] + bg_ref[2 * d:2 * d + 1, cs])
            gi = _sigmoid(g[:, g0 + GATE_BLK:g0 + 2 * GATE_BLK] + bg_ref[2 * d + 1:2 * d + 2, cs])
            neg_log_a = decay[d:d + 1, cs] * r
            a = jnp.exp2(decay_log2[d:d + 1, cs] * r)
            sq = jnp.tanh(neg_log_a) * (a * a + 1.0)
            root = jnp.where(sq > 0.0, sq * lax.rsqrt(sq), 0.0)
            bv = root * (gi * xl)
            if d == 0:
                a_s[:, cs] = a
                b_s[:, cs] = bv
            else:
                mid_ref[:, M_AB + GATE_BLK * j:M_AB + GATE_BLK * (j + 1)] = a
                mid_ref[:, M_BB + GATE_BLK * j:M_BB + GATE_BLK * (j + 1)] = bv

    h_end, p_end = _scan_loop(_row_loader(a_s), _row_loader(b_s), _row_storer(b_s), _row_storer(a_s),
                              sub_len, reverse=False)
    c_in = _chain_states(h_end, p_end, carry_s, reverse=False)
    hend_ref[...] = carry_s[...]
    for j in range(sub_len):
        rows = slice(SUBLANES * j, SUBLANES * (j + 1))
        mid_ref[rows, M_HF:M_HF + D_LRU] = b_s[rows, :] + a_s[rows, :] * c_in

    n_slab = D_CONV // LANES
    for q in range(n_slab):
        ca = slice(D_LRU + q * LANES, D_LRU + (q + 1) * LANES)
        cg = slice(D_LRU + D_CONV + q * LANES, D_LRU + D_CONV + (q + 1) * LANES)
        for s in range(SUBLANES):
            r = s * sub_len
            rows = sub_len + 2 * HALO
            cz_s[q, s * pitch_z:s * pitch_z + rows, :] = (
                pext_s[r:r + rows, ca] * _sigmoid(pext_s[r:r + rows, cg]))
    _interleaved_conv(
        cz_s, n_slab, pitch_z, HALO - CONV_WIDTH // 2, CONV_WIDTH,
        lambda k, q: w8_s[k, :, q * LANES:(q + 1) * LANES],
        lambda q: jnp.broadcast_to(cvb_ref[:, q * LANES:(q + 1) * LANES], (SUBLANES, LANES)),
        yc_s, sub_len)
    acc = yc_s[...]
    mu = jnp.mean(acc, axis=-1, keepdims=True)
    cen = acc - mu
    var = jnp.mean(cen * cen, axis=-1, keepdims=True)
    yc_s[...] = _silu(cen * lax.rsqrt(var + EPS) * cvg_ref[...] + cvbt_ref[...])
    g_cv = _silu(jnp.dot(hn_main[...], w[:, O_CV_G:O_SGU], preferred_element_type=F32))

    def emit_cv(s, q, block):
        rows = slice(s * sub_len, (s + 1) * sub_len)
        cols = slice(q * LANES, (q + 1) * LANES)
        yg_ref[rows, cols] = (block * g_cv[rows, cols]).astype(BF16)

    _to_natural(lambda j, q: yc_s[SUBLANES * j:SUBLANES * (j + 1), q * LANES:(q + 1) * LANES],
                n_slab, u_s, sub_len, pitch_u, emit_cv)

    p_sg = jnp.dot(hn_main[...], w[:, O_SGU:D_IN], preferred_element_type=F32)
    zz = jax.nn.gelu(p_sg[:, 0:2 * D_SGU])
    u = zz[:, 0:D_SGU]
    v = zz[:, D_SGU:2 * D_SGU]
    mu = jnp.mean(v, axis=-1, keepdims=True)
    cen = v - mu
    var = jnp.mean(cen * cen, axis=-1, keepdims=True)
    vn = (cen * lax.rsqrt(var + EPS) * sgg_ref[...] + sgbt_ref[...]).astype(BF16)
    n_ch = tile // CHUNK
    lane = lax.broadcasted_iota(jnp.int32, (CHUNK, n_ch * LANES), 1) % LANES
    even_head = lane < (D_SGU // SGU_HEADS)
    g_sg = _silu(p_sg[:, 2 * D_SGU:3 * D_SGU])
    for k in range(D_SGU // LANES):
        cols = slice(LANES * k, LANES * (k + 1))
        rhs = jnp.concatenate([vn[c * CHUNK:(c + 1) * CHUNK, cols] for c in range(n_ch)], axis=1)
        s_even = jnp.dot(sw_ref[2 * k], rhs, preferred_element_type=F32)
        s_odd = jnp.dot(sw_ref[2 * k + 1], rhs, preferred_element_type=F32)
        sel = jnp.where(even_head, s_even, s_odd)
        for c in range(n_ch):
            rows = slice(c * CHUNK, (c + 1) * CHUNK)
            s_c = sel[:, c * LANES:(c + 1) * LANES] + sbias_ref[:, cols]
            yg_ref[rows, D_CONV + LANES * k:D_CONV + LANES * (k + 1)] = (
                u[rows, cols] * s_c * g_sg[rows, cols]).astype(BF16)


def _pass_a_cast_kernel(*refs, n_in, n_out, **dims):
    ins, outs, scratch = refs[:n_in], refs[n_in:n_in + n_out], refs[n_in + n_out:]
    *a_ins, wi_src, wo_src = ins
    *a_outs, wi_dst, wo_dst = outs
    wi_dst[...] = wi_src[...].astype(BF16)
    wo_dst[...] = wo_src[...].astype(BF16)
    _pass_a_kernel(*a_ins, *a_outs, *scratch, **dims)


def _pass_b_kernel(mid_hbm, yg_ref, x_ref, wo_ref, gate_ref, h0_ref, fg_ref,
                   xo_ref, hend_ref, hb_s, p_s, u_s, yl_s, carry_s, mid_ring, mid_sem,
                   *, tile, n_steps, sub_len, pitch_u, final_norm):
    i = pl.program_id(0)
    wo = wo_ref.at[0]

    def mid_copy(step):
        slot = lax.rem(step, RING_SLOTS)
        row0 = pl.multiple_of((n_steps - 1 - step) * tile, tile)
        return pltpu.make_async_copy(
            mid_hbm.at[pl.ds(row0, tile), :], mid_ring.at[slot], mid_sem.at[slot])

    @pl.when(i == 0)
    def _():
        carry_s[...] = h0_ref[...]
        for step in range(min(RING_SLOTS - 1, n_steps)):
            mid_copy(step).start()

    @pl.when(i + RING_SLOTS - 1 < n_steps)
    def _():
        mid_copy(i + RING_SLOTS - 1).start()

    mid_copy(i).wait()
    mid_ref = mid_ring.at[lax.rem(i, RING_SLOTS)]

    unit = SUBLANES * sub_len
    for base in range(tile - unit, -1, -unit):
        h_end, p_end = _scan_loop(
            _row_loader(mid_ref, M_AB, base), _row_loader(mid_ref, M_BB, base),
            _row_storer(hb_s), _row_storer(p_s), sub_len, reverse=True)
        c_in = _chain_states(h_end, p_end, carry_s, reverse=True)

        def y_step(j, q, base=base, c_in=c_in):
            rows = slice(SUBLANES * j, SUBLANES * (j + 1))
            cols = slice(q * LANES, (q + 1) * LANES)
            hf = mid_ref[base + SUBLANES * j:base + SUBLANES * (j + 1),
                         M_HF + q * LANES:M_HF + (q + 1) * LANES]
            return hf + hb_s[rows, cols] + p_s[rows, cols] * c_in[:, cols]

        def emit(s, q, block, base=base):
            rows = slice(base + s * sub_len, base + (s + 1) * sub_len)
            gl = mid_ref[rows, M_GL + q * LANES:M_GL + (q + 1) * LANES]
            yl_s[rows, q * LANES:(q + 1) * LANES] = (block * gl).astype(BF16)

        _to_natural(y_step, D_LRU // LANES, u_s, sub_len, pitch_u, emit)
    hend_ref[...] = carry_s[...]

    o = jnp.dot(yl_s[...], wo[0:D_LRU, :], preferred_element_type=F32)
    o = o + jnp.dot(yg_ref[...], wo[D_LRU:D_MODEL, :], preferred_element_type=F32)
    xn = x_ref[...] + gate_ref[...] * o
    if final_norm:
        ms = jnp.mean(xn * xn, axis=-1, keepdims=True)
        xn = xn * lax.rsqrt(ms + EPS) * fg_ref[...]
    xo_ref[...] = xn


def _const_spec(shape):
    nd = len(shape)
    return pl.BlockSpec(shape, lambda i: (0,) * nd, pipeline_mode=pl.Buffered(1))


def _mod_call(cvec, w_mod, b_mod):
    depth = w_mod.shape[0]
    bn = 768
    return pl.pallas_call(
        _mod_kernel,
        grid=(depth, 3 * D_MODEL // bn),
        in_specs=[
            pl.BlockSpec((SUBLANES, D_MODEL), lambda l, j: (0, 0)),
            pl.BlockSpec((1, D_MODEL, bn), lambda l, j: (l, 0, j)),
            pl.BlockSpec((1, 1, bn), lambda l, j: (l, 0, j)),
        ],
        out_specs=pl.BlockSpec((1, SUBLANES, bn), lambda l, j: (l, 0, j)),
        out_shape=jax.ShapeDtypeStruct((depth, SUBLANES, 3 * D_MODEL), F32),
        compiler_params=pltpu.CompilerParams(dimension_semantics=("arbitrary", "arbitrary")),
        name="mod_proj",
    )(cvec, w_mod, b_mod.reshape(depth, 1, 3 * D_MODEL))


def _tiling(rows):
    tile = min(TILE, rows)
    assert rows % tile == 0 and tile % (SUBLANES * SUBLANES) == 0 and tile % CHUNK == 0
    sub_len = tile // SUBLANES
    return tile, rows // tile, sub_len


def _pass_a_call(x, mod_row, h0, w_in, p, cast_next=None):
    rows = x.shape[0]
    tile, n, sub_len = _tiling(rows)
    pitch_l = _odd_pitch(sub_len + 2 * LRU_HALO)
    pitch_z = _odd_pitch(sub_len + 2 * HALO)
    pitch_u = _odd_pitch(sub_len)
    ext = tile + 2 * HALO
    hb = tile // HALO
    n_hb = rows // HALO
    row_spec = lambda w: pl.BlockSpec((tile, w), lambda i: (i, 0))
    consts = [p["lru_conv_w"], p["lru_conv_b"], p["lru_lam"],
              p["w_gate"], p["b_gate"], p["cv_w"], p["cv_b"], p["cv_ln_g"], p["cv_ln_b"],
              p["sgu_ln_g"], p["sgu_ln_b"], p["sgu_w"], p["sgu_bias"], h0]
    dims = dict(tile=tile, sub_len=sub_len, pitch_l=pitch_l, pitch_z=pitch_z, pitch_u=pitch_u)
    operands = [x, x, mod_row, p["norm_g"], w_in, *consts]
    in_specs = [
        row_spec(D_MODEL),
        pl.BlockSpec((HALO, D_MODEL), lambda i: (jnp.minimum((i + 1) * hb, n_hb - 1), 0)),
        _const_spec(mod_row.shape), _const_spec(p["norm_g"].shape), _const_spec(w_in.shape),
    ] + [_const_spec(c.shape) for c in consts]
    out_specs = [row_spec(D_MID), row_spec(D_LRU), pl.BlockSpec((1, D_LRU), lambda i: (0, 0))]
    out_shape = [
        jax.ShapeDtypeStruct((rows, D_MID), F32),
        jax.ShapeDtypeStruct((rows, D_LRU), BF16),
        jax.ShapeDtypeStruct((1, D_LRU), F32),
    ]
    kern = functools.partial(_pass_a_kernel, **dims)
    if cast_next is not None:
        *stacks, nxt = cast_next
        for wgt in stacks:
            k_rows, k_cols = wgt.shape[1:]
            assert k_rows % n == 0 and (k_rows // n) % (2 * SUBLANES) == 0
            chunk = pl.BlockSpec((1, k_rows // n, k_cols), lambda i: (nxt, i, 0))
            operands.append(wgt)
            in_specs.append(chunk)
            out_specs.append(pl.BlockSpec((1, k_rows // n, k_cols), lambda i: (0, i, 0)))
            out_shape.append(jax.ShapeDtypeStruct((1, k_rows, k_cols), BF16))
        kern = functools.partial(_pass_a_cast_kernel, n_in=len(in_specs), n_out=len(out_specs),
                                 **dims)
    return pl.pallas_call(
        kern,
        grid=(n,),
        in_specs=in_specs,
        out_specs=out_specs,
        out_shape=out_shape,
        scratch_shapes=[
            pltpu.VMEM((tile + HALO, D_MODEL), BF16),
            pltpu.VMEM((ext, D_EXT), F32),
            pltpu.VMEM((D_LRU // LANES, SUBLANES * pitch_l, LANES), F32),
            pltpu.VMEM((tile, D_LRU), F32),
            pltpu.VMEM((tile, D_LRU), F32),
            pltpu.VMEM((tile, D_LRU), F32),
            pltpu.VMEM((D_CONV // LANES, SUBLANES * pitch_z, LANES), F32),
            pltpu.VMEM((CONV_WIDTH, SUBLANES, D_CONV), F32),
            pltpu.VMEM((tile, D_CONV), F32),
            pltpu.VMEM((D_CONV // LANES, SUBLANES * pitch_u, LANES), F32),
            pltpu.VMEM((1, D_LRU), F32),
        ],
        compiler_params=pltpu.CompilerParams(
            dimension_semantics=("arbitrary",), vmem_limit_bytes=VMEM_LIMIT),
        name="pass_a",
    )(*operands)


def _pass_b_call(mid, yg, x, gate_row, h0, w_out, final_g, final_norm):
    rows = x.shape[0]
    unit, _, sub_len = _tiling(rows)
    tile = min(TILE_B, rows)
    assert rows % tile == 0 and tile % unit == 0
    n = rows // tile
    pitch_u = _odd_pitch(sub_len)
    row_spec = lambda w: pl.BlockSpec((tile, w), lambda i: (n - 1 - i, 0))
    consts = [gate_row, h0, final_g]
    kern = functools.partial(_pass_b_kernel, tile=tile, n_steps=n, sub_len=sub_len, pitch_u=pitch_u,
                             final_norm=final_norm)
    return pl.pallas_call(
        kern,
        grid=(n,),
        in_specs=[pl.BlockSpec(memory_space=pl.ANY), row_spec(D_LRU), row_spec(D_MODEL),
                  _const_spec(w_out.shape)]
        + [_const_spec(c.shape) for c in consts],
        out_specs=[row_spec(D_MODEL), pl.BlockSpec((1, D_LRU), lambda i: (0, 0))],
        out_shape=[jax.ShapeDtypeStruct((rows, D_MODEL), F32),
                   jax.ShapeDtypeStruct((1, D_LRU), F32)],
        scratch_shapes=[
            pltpu.VMEM((unit, D_LRU), F32),
            pltpu.VMEM((unit, D_LRU), F32),
            pltpu.VMEM((D_LRU // LANES, SUBLANES * pitch_u, LANES), F32),
            pltpu.VMEM((tile, D_LRU), BF16),
            pltpu.VMEM((1, D_LRU), F32),
            pltpu.VMEM((RING_SLOTS, tile, D_MID), F32),
            pltpu.SemaphoreType.DMA((RING_SLOTS,)),
        ],
        compiler_params=pltpu.CompilerParams(
            dimension_semantics=("arbitrary",), vmem_limit_bytes=VMEM_LIMIT_B),
        name="pass_b",
    )(mid, yg, x, w_out, *consts)


def _layer_params(l, norm_g, lru_conv_w, lru_conv_b, lru_lam, lru_w_r, lru_b_r, lru_w_i, lru_b_i,
                  cv_w, cv_b, cv_ln_g, cv_ln_b, sgu_ln_g, sgu_ln_b, sgu_w, sgu_b):
    gates = jnp.stack([lru_w_r[l, 0], lru_w_i[l, 0], lru_w_r[l, 1], lru_w_i[l, 1]])
    hpb = GATE_BLK // LRU_HEAD_DIM
    nb = D_LRU // GATE_BLK
    gates = gates.reshape(4, nb, hpb, LRU_HEAD_DIM, LRU_HEAD_DIM)
    bd = jnp.einsum("tjqio,qr->tjqiro", gates, jnp.eye(hpb, dtype=gates.dtype))
    bd = bd.reshape(4, nb, GATE_BLK, GATE_BLK).transpose(1, 2, 0, 3).reshape(nb, GATE_BLK, 4 * GATE_BLK)
    b_gate = jnp.stack([lru_b_r[l, 0], lru_b_i[l, 0], lru_b_r[l, 1], lru_b_i[l, 1]]).reshape(4, D_LRU)
    return dict(
        norm_g=norm_g[l].reshape(1, D_MODEL),
        lru_conv_w=lru_conv_w[l],
        lru_conv_b=lru_conv_b[l].reshape(1, D_LRU),
        lru_lam=lru_lam[l],
        w_gate=bd.astype(BF16),
        b_gate=b_gate,
        cv_w=cv_w[l],
        cv_b=cv_b[l].reshape(1, D_CONV),
        cv_ln_g=cv_ln_g[l].reshape(1, D_CONV),
        cv_ln_b=cv_ln_b[l].reshape(1, D_CONV),
        sgu_ln_g=sgu_ln_g[l].reshape(1, D_SGU),
        sgu_ln_b=sgu_ln_b[l].reshape(1, D_SGU),
        sgu_w=sgu_w[l].astype(BF16),
        sgu_bias=jnp.repeat(sgu_b[l].T, D_SGU // SGU_HEADS, axis=1),
    )


def kernel(x, c, ctx, c_ctx, w_mod, b_mod, norm_g, w_in, w_out, lru_conv_w, lru_conv_b, lru_lam,
           lru_w_r, lru_b_r, lru_w_i, lru_b_i, cv_w, cv_b, cv_ln_g, cv_ln_b, sgu_ln_g, sgu_ln_b,
           sgu_w, sgu_b, final_g):
    assert x.shape[0] == 1 and c.shape[0] == 1 and ctx.shape[0] == 1
    depth = w_mod.shape[0]
    xs = x[0]
    cs = ctx[0]
    cvec = jnp.concatenate(
        [c, c_ctx.reshape(1, D_MODEL), jnp.zeros((SUBLANES - 2, D_MODEL), F32)], axis=0)
    mod = _mod_call(cvec, w_mod, b_mod)
    fg = final_g.reshape(1, D_MODEL)
    zero_state = jnp.zeros((1, D_LRU), F32)
    w_in_bf = w_in[0:1].astype(BF16)
    w_out_bf = w_out[0:1].astype(BF16)
    for l in range(depth):
        last = l == depth - 1
        p = _layer_params(l, norm_g, lru_conv_w, lru_conv_b, lru_lam, lru_w_r, lru_b_r, lru_w_i,
                          lru_b_i, cv_w, cv_b, cv_ln_g, cv_ln_b, sgu_ln_g, sgu_ln_b, sgu_w, sgu_b)
        mod_x = mod[l, 0:1]
        mod_c = mod[l, 1:2]
        mid_c, yg_c, hf_end = _pass_a_call(cs, mod_c[:, 0:2 * D_MODEL], zero_state, w_in_bf, p)
        cs_new, hb_end = _pass_b_call(mid_c, yg_c, cs, mod_c[:, 2 * D_MODEL:], zero_state, w_out_bf,
                                      fg, False)
        mid_x, yg_x, _, *next_bf = _pass_a_call(
            xs, mod_x[:, 0:2 * D_MODEL], hf_end, w_in_bf, p,
            cast_next=None if last else (w_in, w_out, l + 1))
        xs, _ = _pass_b_call(mid_x, yg_x, xs, mod_x[:, 2 * D_MODEL:], hb_end, w_out_bf, fg, last)
        cs = cs_new
        if not last:
            w_in_bf, w_out_bf = next_bf
    return xs[None]
```
